```python
import jax, jax.numpy as jnp
from jax import lax
import numpy as np

D_MODEL = 4096
BATCH = 1
SEQ = 16384
DEPTH = 2

GRID_W = 64
ROPE_THETA = 10000.0
NORM_EPS = 1e-6
Q_BLOCK = 128
N_EVEN = (DEPTH + 1) // 2
N_ODD = DEPTH // 2

A_WIDTH = D_MODEL // 2
A_HEADS = 16
A_BLK = A_WIDTH // A_HEADS
CONV_W = 4
CONV_LEFT = 2
RG_C = 8.0

B_HEADS = 16
B_NOPE = 128
B_ROPE = 64
B_V = 128
B_QLORA = 1024
B_KVLORA = 512

C_HEADS = 16
C_KV_HEADS = 4
C_HEAD_DIM = 128

D_WIDTH = D_MODEL // 2
D_GROUPS = 16
D_GDIM = D_WIDTH // D_GROUPS
D_CHUNK = 128

N_EXPERTS = 16
EC_CAPACITY = 2
F_EXPERT = 1536

EVEN_SIZES = (A_WIDTH, A_WIDTH, B_QLORA, B_KVLORA, B_ROPE)
EVEN_IN = sum(EVEN_SIZES)
EVEN_OUT = A_WIDTH + B_HEADS * B_V
ODD_SIZES = (C_HEADS * C_HEAD_DIM, C_KV_HEADS * C_HEAD_DIM, C_KV_HEADS * C_HEAD_DIM, D_WIDTH, D_WIDTH)
ODD_IN = sum(ODD_SIZES)
ODD_OUT = C_HEADS * C_HEAD_DIM + D_WIDTH

kernel_name = "hybrid_rglru_mla_gqa_sgu_ecmoe"

F32 = jnp.float32


def rms_norm(x, g):
    xf = x.astype(F32)
    y = xf * lax.rsqrt(jnp.mean(xf * xf, axis=-1, keepdims=True) + NORM_EPS)
    return (y * g.astype(F32)).astype(x.dtype)


def split_cols(z, sizes):
    out, start = [], 0
    for n in sizes:
        out.append(z[..., start:start + n])
        start += n
    return out


def axial_rope_tables(seq, rot_dim):
    rows = seq // GRID_W
    row = jnp.repeat(jnp.arange(rows, dtype=F32), GRID_W)
    col = jnp.tile(jnp.arange(GRID_W, dtype=F32), rows)
    n = rot_dim // 4
    freqs = jnp.power(ROPE_THETA, -jnp.arange(n, dtype=F32) / n)
    ang = jnp.concatenate([row[:, None] * freqs, col[:, None] * freqs], axis=-1)
    return jnp.cos(ang), jnp.sin(ang)


def apply_rope(x, cos, sin):
    half = x.shape[-1] // 2
    x1 = x[..., :half].astype(F32)
    x2 = x[..., half:].astype(F32)
    c = cos[None, :, None, :]
    s = sin[None, :, None, :]
    return jnp.concatenate([x1 * c - x2 * s, x1 * s + x2 * c], axis=-1).astype(x.dtype)


def block_attention(q, k, v):
    bsz, seq, heads, dk = q.shape
    hk = k.shape[2]
    grp = heads // hk
    scale = dk ** -0.5
    qb = q.reshape(bsz, seq // Q_BLOCK, Q_BLOCK, hk, grp, dk).transpose(1, 0, 2, 3, 4, 5)

    def one_block(qblk):
        s = jnp.einsum('bqkgd,bskd->bkgqs', qblk, k).astype(F32) * scale
        p = jax.nn.softmax(s, axis=-1).astype(v.dtype)
        return jnp.einsum('bkgqs,bskd->bqkgd', p, v)

    o = lax.map(one_block, qb)
    return o.transpose(1, 0, 2, 3, 4, 5).reshape(bsz, seq, heads, v.shape[-1])


def _lin_combine(earlier, later):
    a_e, b_e = earlier
    a_l, b_l = later
    return a_e * a_l, a_l * b_e + b_l


def rglru_mixer(xa, ga, conv_w, conv_b, gate_w, gate_b, lam):
    bsz, seq, _ = xa.shape
    xp = jnp.pad(xa, ((0, 0), (CONV_LEFT, CONV_W - 1 - CONV_LEFT), (0, 0)))
    xc = conv_b + sum(xp[:, j:j + seq, :] * conv_w[j] for j in range(CONV_W))
    xblk = xc.reshape(bsz, seq, A_HEADS, A_BLK)
    gates = jnp.einsum('bshi,dghij->dgbshj', xblk, gate_w).reshape(2, 2, bsz, seq, A_WIDTH)
    gates = gates.astype(F32) + gate_b.astype(F32)[:, :, None, None, :]
    r = jax.nn.sigmoid(gates[:, 0])
    i = jax.nn.sigmoid(gates[:, 1])
    log_a = -RG_C * r * jax.nn.softplus(-lam.astype(F32))[:, None, None, :]
    a = jnp.exp(log_a)
    b = jnp.sqrt(-jnp.expm1(2.0 * log_a)) * (i * xc.astype(F32)[None])
    _, h_fwd = lax.associative_scan(_lin_combine, (a[0], b[0]), axis=1)
    _, h_bwd = lax.associative_scan(_lin_combine, (a[1], b[1]), axis=1, reverse=True)
    return (jax.nn.gelu(ga.astype(F32)) * (h_fwd + h_bwd)).astype(xa.dtype)


def mla_mixer(cq, ckv, kr, cq_norm, ckv_norm, w_q_up, w_kv_up, q_norm, k_norm, cos, sin):
    bsz, seq, _ = cq.shape
    q = (rms_norm(cq, cq_norm) @ w_q_up).reshape(bsz, seq, B_HEADS, B_NOPE + B_ROPE)
    kv = (rms_norm(ckv, ckv_norm) @ w_kv_up).reshape(bsz, seq, B_HEADS, B_NOPE + B_V)
    k_nope, v = kv[..., :B_NOPE], kv[..., B_NOPE:]
    k_rope = jnp.broadcast_to(kr[:, :, None, :], (bsz, seq, B_HEADS, B_ROPE))
    k = jnp.concatenate([k_nope, k_rope], axis=-1)
    q = rms_norm(q, q_norm)
    k = rms_norm(k, k_norm)
    q = jnp.concatenate([q[..., :B_NOPE], apply_rope(q[..., B_NOPE:], cos, sin)], axis=-1)
    k = jnp.concatenate([k[..., :B_NOPE], apply_rope(k[..., B_NOPE:], cos, sin)], axis=-1)
    o = block_attention(q, k, v)
    return o.reshape(bsz, seq, B_HEADS * B_V)


def gqa_mixer(xq, xk, xv, q_norm, k_norm, cos, sin):
    bsz, seq, _ = xq.shape
    q = xq.reshape(bsz, seq, C_HEADS, C_HEAD_DIM)
    k = xk.reshape(bsz, seq, C_KV_HEADS, C_HEAD_DIM)
    v = xv.reshape(bsz, seq, C_KV_HEADS, C_HEAD_DIM)
    q = apply_rope(rms_norm(q, q_norm), cos, sin)
    k = apply_rope(rms_norm(k, k_norm), cos, sin)
    o = block_attention(q, k, v)
    return o.reshape(bsz, seq, C_HEADS * C_HEAD_DIM)


def sgu_mixer(u, v, v_norm, w_s, b_s):
    bsz, seq, _ = u.shape
    u = jax.nn.gelu(u)
    v = rms_norm(jax.nn.gelu(v), v_norm)
    vc = v.reshape(bsz, seq // D_CHUNK, D_CHUNK, D_GROUPS, D_GDIM)
    mixed = jnp.einsum('gpq,bcqgd->bcpgd', w_s, vc) + b_s.T[None, None, :, :, None]
    return u * mixed.reshape(bsz, seq, D_WIDTH)


def expert_choice_ffn(h, w_router, w_gate, w_up, w_down):
    bsz, seq, dm = h.shape
    cap = EC_CAPACITY * seq // N_EXPERTS
    aff = jax.nn.softmax(jnp.einsum('bsd,de->bse', h, w_router).astype(F32), axis=-1)
    g, idx = lax.top_k(aff.transpose(0, 2, 1), cap)
    xin = jax.vmap(lambda hb, ib: hb[ib])(h, idx)
    hg = jnp.einsum('becd,edf->becf', xin, w_gate)
    hu = jnp.einsum('becd,edf->becf', xin, w_up)
    y = jnp.einsum('becf,efd->becd', jax.nn.silu(hg) * hu, w_down)
    y = y * g[..., None].astype(y.dtype)
    return jax.vmap(lambda yb, ib: jax.ops.segment_sum(yb.reshape(-1, dm), ib.reshape(-1), num_segments=seq))(y, idx)


def setup_inputs(seed: int = 0) -> dict:
    key = jax.random.key(seed)
    ks = iter(jax.random.split(key, 32))

    def nrm(shape, scale):
        return jax.random.normal(next(ks), shape, F32) * scale

    def gain(shape):
        return 1.0 + 0.02 * jax.random.normal(next(ks), shape, F32)

    u = jax.random.uniform(next(ks), (N_EVEN, 2, A_WIDTH), F32, 0.9, 0.999)
    a_base = u ** (1.0 / RG_C)
    a_lambda = jnp.log(a_base) - jnp.log1p(-a_base)
    return {
        'x': nrm((BATCH, SEQ, D_MODEL), 1.0),
        'e_norm': gain((N_EVEN, D_MODEL)),
        'e_w_in': nrm((N_EVEN, D_MODEL, EVEN_IN), D_MODEL ** -0.5),
        'a_conv_w': nrm((N_EVEN, CONV_W, A_WIDTH), CONV_W ** -0.5),
        'a_conv_b': nrm((N_EVEN, A_WIDTH), 0.01),
        'a_gate_w': nrm((N_EVEN, 2, 2, A_HEADS, A_BLK, A_BLK), A_BLK ** -0.5),
        'a_gate_b': nrm((N_EVEN, 2, 2, A_WIDTH), 0.01),
        'a_lambda': a_lambda,
        'b_cq_norm': gain((N_EVEN, B_QLORA)),
        'b_ckv_norm': gain((N_EVEN, B_KVLORA)),
        'b_w_q_up': nrm((N_EVEN, B_QLORA, B_HEADS * (B_NOPE + B_ROPE)), B_QLORA ** -0.5),
        'b_w_kv_up': nrm((N_EVEN, B_KVLORA, B_HEADS * (B_NOPE + B_V)), B_KVLORA ** -0.5),
        'b_q_norm': gain((N_EVEN, B_NOPE + B_ROPE)),
        'b_k_norm': gain((N_EVEN, B_NOPE + B_ROPE)),
        'e_w_out': nrm((N_EVEN, EVEN_OUT, D_MODEL), EVEN_OUT ** -0.5),
        'o_norm': gain((N_ODD, D_MODEL)),
        'o_w_in': nrm((N_ODD, D_MODEL, ODD_IN), D_MODEL ** -0.5),
        'c_q_norm': gain((N_ODD, C_HEAD_DIM)),
        'c_k_norm': gain((N_ODD, C_HEAD_DIM)),
        'd_v_norm': gain((N_ODD, D_WIDTH)),
        'd_w_s': nrm((N_ODD, D_GROUPS, D_CHUNK, D_CHUNK), D_CHUNK ** -0.5),
        'd_b_s': nrm((N_ODD, D_GROUPS, D_CHUNK), 0.01),
        'o_w_out': nrm((N_ODD, ODD_OUT, D_MODEL), ODD_OUT ** -0.5),
        'f_norm': gain((DEPTH, D_MODEL)),
        'w_router': nrm((DEPTH, D_MODEL, N_EXPERTS), D_MODEL ** -0.5),
        'w_gate': nrm((DEPTH, N_EXPERTS, D_MODEL, F_EXPERT), D_MODEL ** -0.5),
        'w_up': nrm((DEPTH, N_EXPERTS, D_MODEL, F_EXPERT), D_MODEL ** -0.5),
        'w_down': nrm((DEPTH, N_EXPERTS, F_EXPERT, D_MODEL), F_EXPERT ** -0.5),
    }


def reference(x, e_norm, e_w_in, a_conv_w, a_conv_b, a_gate_w, a_gate_b, a_lambda,
              b_cq_norm, b_ckv_norm, b_w_q_up, b_w_kv_up, b_q_norm, b_k_norm, e_w_out,
              o_norm, o_w_in, c_q_norm, c_k_norm, d_v_norm, d_w_s, d_b_s, o_w_out,
              f_norm, w_router, w_gate, w_up, w_down):
    seq = x.shape[1]
    cos_b, sin_b = axial_rope_tables(seq, B_ROPE)
    cos_c, sin_c = axial_rope_tables(seq, C_HEAD_DIM)
    for layer in range(DEPTH):
        j = layer // 2
        if layer % 2 == 0:
            z = rms_norm(x, e_norm[j]) @ e_w_in[j]
            xa, ga, cq, ckv, kr = split_cols(z, EVEN_SIZES)
            ya = rglru_mixer(xa, ga, a_conv_w[j], a_conv_b[j], a_gate_w[j], a_gate_b[j], a_lambda[j])
            yb = mla_mixer(cq, ckv, kr, b_cq_norm[j], b_ckv_norm[j], b_w_q_up[j], b_w_kv_up[j],
                           b_q_norm[j], b_k_norm[j], cos_b, sin_b)
            x = x + jnp.concatenate([ya, yb], axis=-1) @ e_w_out[j]
        else:
            z = rms_norm(x, o_norm[j]) @ o_w_in[j]
            xq, xk, xv, du, dv = split_cols(z, ODD_SIZES)
            yc = gqa_mixer(xq, xk, xv, c_q_norm[j], c_k_norm[j], cos_c, sin_c)
            yd = sgu_mixer(du, dv, d_v_norm[j], d_w_s[j], d_b_s[j])
            x = x + jnp.concatenate([yc, yd], axis=-1) @ o_w_out[j]
        x = x + expert_choice_ffn(rms_norm(x, f_norm[layer]), w_router[layer], w_gate[layer],
                                  w_up[layer], w_down[layer])
    return x
```

```python
import functools

import jax
import jax.numpy as jnp
from jax import lax
from jax.experimental import pallas as pl
from jax.experimental.pallas import tpu as pltpu

F32 = jnp.float32
BF16 = jnp.bfloat16

GRID_W = 64
ROPE_THETA = 10000.0
NORM_EPS = 1e-6
A_HEADS = 16
A_BLK = 128
CONV_W = 4
RG_C = 8.0
B_HEADS = 16
B_NOPE = 128
B_ROPE = 64
B_V = 128
B_QLORA = 1024
B_KVLORA = 512
C_HEADS = 16
C_KV_HEADS = 4
C_HEAD_DIM = 128
D_GROUPS = 16
D_CHUNK = 128
N_EXPERTS = 16
EC_CAPACITY = 2

V7X_LANES = 128
V7X_SUBLANES = 8
V7X_MXU_DIM = 256
V7X_VMEM_BYTES = 64 * 1024 * 1024
MIB = 1024 * 1024


def _cparams(dims, vmem_mib):
    assert vmem_mib * MIB < V7X_VMEM_BYTES
    return pltpu.CompilerParams(dimension_semantics=dims, vmem_limit_bytes=vmem_mib * MIB)


def _rms_rows(x, g):
    ms = jnp.mean(x * x, axis=-1, keepdims=True)
    return x * lax.rsqrt(ms + NORM_EPS) * g


def _gelu_tanh(x):
    return 0.5 * x * (1.0 + jnp.tanh(0.7978845608028654 * (x + 0.044715 * (x * x * x))))


def _norm_mm_kernel(x_ref, g_ref, w_ref, o_ref, xn_ref, *, tm, rows):
    @pl.when(pl.program_id(1) == 0)
    def _():
        g = g_ref[...]

        def body(c, carry):
            r0 = pl.multiple_of(c * rows, rows)
            xn_ref[pl.ds(r0, rows), :] = _rms_rows(x_ref[pl.ds(r0, rows), :], g).astype(BF16)
            return carry

        lax.fori_loop(0, tm // rows, body, 0)

    o_ref[...] = jnp.dot(xn_ref[...], w_ref[...], preferred_element_type=F32)


def _norm_mm(x, g, w, *, tm=512, tn=512):
    s, d = x.shape
    n = w.shape[1]
    return pl.pallas_call(
        functools.partial(_norm_mm_kernel, tm=tm, rows=64),
        out_shape=jax.ShapeDtypeStruct((s, n), F32),
        grid=(s // tm, n // tn),
        in_specs=[
            pl.BlockSpec((tm, d), lambda i, j: (i, 0)),
            pl.BlockSpec((1, d), lambda i, j: (0, 0)),
            pl.BlockSpec((d, tn), lambda i, j: (0, j)),
        ],
        out_specs=pl.BlockSpec((tm, tn), lambda i, j: (i, j)),
        scratch_shapes=[pltpu.VMEM((tm, d), BF16)],
        compiler_params=_cparams(("parallel", "arbitrary"), 40),
        name="norm_mm",
    )(x, g.reshape(1, d), w)


def _mm2_res_kernel(a1_ref, a2_ref, w1_ref, w2_ref, r_ref, o_ref):
    acc = jnp.dot(a1_ref[...], w1_ref[...], preferred_element_type=F32)
    acc = acc + jnp.dot(a2_ref[...], w2_ref[...], preferred_element_type=F32)
    o_ref[...] = r_ref[...] + acc


def _mm2_res(a1, a2, w1, w2, res, *, tm=512, tn=1024):
    s, k1 = a1.shape
    k2 = a2.shape[1]
    n = w1.shape[1]
    return pl.pallas_call(
        _mm2_res_kernel,
        out_shape=jax.ShapeDtypeStruct((s, n), F32),
        grid=(s // tm, n // tn),
        in_specs=[
            pl.BlockSpec((tm, k1), lambda i, j: (i, 0)),
            pl.BlockSpec((tm, k2), lambda i, j: (i, 0)),
            pl.BlockSpec((k1, tn), lambda i, j: (0, j)),
            pl.BlockSpec((k2, tn), lambda i, j: (0, j)),
            pl.BlockSpec((tm, tn), lambda i, j: (i, j)),
        ],
        out_specs=pl.BlockSpec((tm, tn), lambda i, j: (i, j)),
        compiler_params=_cparams(("parallel", "arbitrary"), 40),
        name="mm2_res",
    )(a1, a2, w1, w2, res)


def _rglru_kernel(*refs, reverse, final, t, c):
    if final:
        (xa_ref, xp_ref, xn_ref, cw_ref, cb_ref, gw_ref, gb_ref, lam_ref, ga_ref, hf_ref,
         o_ref, a_scr, b_scr, carry) = refs
    else:
        (xa_ref, xp_ref, xn_ref, cw_ref, cb_ref, gw_ref, gb_ref, lam_ref,
         o_ref, a_scr, b_scr, carry) = refs
    i = pl.program_id(0)
    n_t = pl.num_programs(0)
    ti = (n_t - 1 - i) if reverse else i

    @pl.when(i == 0)
    def _():
        carry[...] = jnp.zeros_like(carry)

    x = xa_ref[...]
    xprev = jnp.where(ti > 0, xp_ref[...], 0.0)
    xnext = jnp.where(ti < n_t - 1, xn_ref[...], 0.0)
    xe = jnp.concatenate([xprev, x, xnext], axis=0)
    n = t + 2 * V7X_SUBLANES
    cw = cw_ref[...]
    xc = cb_ref[...] + cw[2:3] * x
    for j, sh in ((0, 2), (1, 1), (3, n - 1)):
        xc = xc + cw[j:j + 1] * pltpu.roll(xe, sh, 0)[V7X_SUBLANES:V7X_SUBLANES + t]
    a_scr[...] = xc

    z = -lam_ref[0]
    sp = jnp.maximum(z, 0.0) + jnp.log1p(jnp.exp(-jnp.abs(z)))
    gb = gb_ref[0]
    row = lax.broadcasted_iota(jnp.int32, (t, A_BLK), 0) & (V7X_SUBLANES - 1)
    for h in range(c // A_BLK):
        sl = slice(h * A_BLK, (h + 1) * A_BLK)
        xch = a_scr[:, sl]
        g2 = jnp.dot(xch.astype(BF16), gw_ref[0, h], preferred_element_type=F32)
        r = jax.nn.sigmoid(g2[:, :A_BLK] + gb[0:1, sl])
        ig = jax.nn.sigmoid(g2[:, A_BLK:] + gb[1:2, sl])
        log_a = (-RG_C) * r * sp[:, sl]
        a = jnp.exp(log_a)
        th = jnp.tanh(log_a)
        b = jnp.sqrt(-2.0 * th / (1.0 - th)) * (ig * xch)
        for s in (1, 2, 4):
            if reverse:
                a_s = pltpu.roll(a, t - s, 0)
                b_s = pltpu.roll(b, t - s, 0)
                m = row < V7X_SUBLANES - s
            else:
                a_s = pltpu.roll(a, s, 0)
                b_s = pltpu.roll(b, s, 0)
                m = row >= s
            b = jnp.where(m, a * b_s + b, b)
            a = jnp.where(m, a * a_s, a)
        a_scr[:, sl] = a
        b_scr[:, sl] = b

    nb = t // V7X_SUBLANES

    def blk(k, h):
        kk = (nb - 1 - k) if reverse else k
        r0 = pl.multiple_of(kk * V7X_SUBLANES, V7X_SUBLANES)
        hh = a_scr[pl.ds(r0, V7X_SUBLANES), :] * h + b_scr[pl.ds(r0, V7X_SUBLANES), :]
        b_scr[pl.ds(r0, V7X_SUBLANES), :] = hh
        edge = hh[0:1] if reverse else hh[V7X_SUBLANES - 1:V7X_SUBLANES]
        return jnp.broadcast_to(edge, (V7X_SUBLANES, c))

    carry[...] = lax.fori_loop(0, nb, blk, carry[...])

    if final:
        o_ref[...] = (_gelu_tanh(ga_ref[...]) * (hf_ref[...] + b_scr[...])).astype(o_ref.dtype)
    else:
        o_ref[...] = b_scr[...]


def _rglru_dir(z, cw, cb, gw, gb, lam, *, reverse, hf=None, t=256):
    s = z.shape[0]
    c = cw.shape[1]
    n_t = s // t
    t8 = t // V7X_SUBLANES
    last8 = s // V7X_SUBLANES - 1
    d = 1 if reverse else 0
    final = hf is not None

    def tmap(i):
        return (n_t - 1 - i) if reverse else i

    in_specs = [
        pl.BlockSpec((t, c), lambda i: (tmap(i), 0)),
        pl.BlockSpec((V7X_SUBLANES, c), lambda i: (jnp.maximum(tmap(i) * t8 - 1, 0), 0)),
        pl.BlockSpec((V7X_SUBLANES, c), lambda i: (jnp.minimum((tmap(i) + 1) * t8, last8), 0)),
        pl.BlockSpec((CONV_W, c), lambda i: (0, 0)),
        pl.BlockSpec((1, c), lambda i: (0, 0)),
        pl.BlockSpec((1, c // A_BLK, A_BLK, 2 * A_BLK), lambda i: (d, 0, 0, 0)),
        pl.BlockSpec((1, 2, c), lambda i: (d, 0, 0)),
        pl.BlockSpec((1, 1, c), lambda i: (d, 0, 0)),
    ]
    args = [z, z, z, cw, cb, gw, gb, lam.reshape(2, 1, c)]
    if final:
        in_specs += [pl.BlockSpec((t, c), lambda i: (tmap(i), 1)),
                     pl.BlockSpec((t, c), lambda i: (tmap(i), 0))]
        args += [z, hf]
    return pl.pallas_call(
        functools.partial(_rglru_kernel, reverse=reverse, final=final, t=t, c=c),
        out_shape=jax.ShapeDtypeStruct((s, c), BF16 if final else F32),
        grid=(n_t,),
        in_specs=in_specs,
        out_specs=pl.BlockSpec((t, c), lambda i: (tmap(i), 0)),
        scratch_shapes=[pltpu.VMEM((t, c), F32), pltpu.VMEM((t, c), F32),
                        pltpu.VMEM((V7X_SUBLANES, c), F32)],
        compiler_params=_cparams(("arbitrary",), 40),
        name="rglru_bwd" if reverse else "rglru_fwd",
    )(*args)


def _mla_prep_kernel(cq_ref, ckv_ref, kr_ref, cqn_ref, ckvn_ref, wq_ref, wkv_ref, gq_ref, gk_ref,
                     cos_ref, sa_ref, sb_ref, q_out, k_out, vt_out, q_scr, kv_scr, *, scale):
    hd = 2 * B_NOPE
    inv_dim = 1.0 / (B_NOPE + B_ROPE)
    cqn = _rms_rows(cq_ref[...], cqn_ref[...]).astype(BF16)
    q_scr[...] = jnp.dot(cqn, wq_ref[...], preferred_element_type=F32)
    ckvn = _rms_rows(ckv_ref[...], ckvn_ref[...]).astype(BF16)
    kv_scr[...] = jnp.dot(ckvn, wkv_ref[...], preferred_element_type=F32)

    cos_t = cos_ref[...]
    sin_a = sa_ref[...]
    sin_b = sb_ref[...]

    def rope(rp):
        return rp * cos_t + pltpu.roll(rp, 96, 1) * sin_a + pltpu.roll(rp, 32, 1) * sin_b

    gq = gq_ref[...]
    gk = gk_ref[...]
    krp = kr_ref[...]
    ss_kr = jnp.sum(krp * krp, axis=-1, keepdims=True)
    krr = rope(krp * gk[:, B_NOPE:])
    for h in range(B_HEADS):
        qh = q_scr[:, h * hd:(h + 1) * hd]
        fq = lax.rsqrt(jnp.sum(qh * qh, axis=-1, keepdims=True) * inv_dim + NORM_EPS) * scale
        qn = qh * fq * gq
        q_out[h, 0] = jnp.concatenate([qn[:, :B_NOPE], rope(qn[:, B_NOPE:])], axis=-1).astype(BF16)
        kn = kv_scr[:, h * hd:h * hd + B_NOPE]
        v = kv_scr[:, h * hd + B_NOPE:(h + 1) * hd]
        fk = lax.rsqrt((jnp.sum(kn * kn, axis=-1, keepdims=True) + ss_kr) * inv_dim + NORM_EPS)
        k_out[h, 0] = jnp.concatenate([kn * fk * gk[:, :B_NOPE], krr * fk], axis=-1).astype(BF16)
        vt_out[h, 0] = v.T.astype(BF16)


def _mla_prep(z, cqn, ckvn, wq, wkv, gq, gk, cos_t, sin_a, sin_b, *, scale, tk, t=256):
    s = z.shape[0]
    hd = 2 * B_NOPE
    r = tk // t
    c0 = 2 * (A_HEADS * A_BLK)
    cq_blk = c0 // B_QLORA
    ckv_blk = (c0 + B_QLORA) // B_KVLORA
    kr_blk = (c0 + B_QLORA + B_KVLORA) // V7X_LANES
    const = lambda i: (0, 0)
    return pl.pallas_call(
        functools.partial(_mla_prep_kernel, scale=scale),
        out_shape=(
            jax.ShapeDtypeStruct((B_HEADS, 1, s, hd), BF16),
            jax.ShapeDtypeStruct((B_HEADS, s // tk, tk, hd), BF16),
            jax.ShapeDtypeStruct((B_HEADS, s // tk, B_V, tk), BF16),
        ),
        grid=(s // t,),
        in_specs=[
            pl.BlockSpec((t, B_QLORA), lambda i: (i, cq_blk)),
            pl.BlockSpec((t, B_KVLORA), lambda i: (i, ckv_blk)),
            pl.BlockSpec((t, V7X_LANES), lambda i: (i, kr_blk)),
            pl.BlockSpec((1, B_QLORA), const),
            pl.BlockSpec((1, B_KVLORA), const),
            pl.BlockSpec((B_QLORA, B_HEADS * hd), const),
            pl.BlockSpec((B_KVLORA, B_HEADS * hd), const),
            pl.BlockSpec((1, hd), const),
            pl.BlockSpec((1, hd), const),
            pl.BlockSpec((t, V7X_LANES), lambda i: (i, 0)),
            pl.BlockSpec((t, V7X_LANES), lambda i: (i, 0)),
            pl.BlockSpec((t, V7X_LANES), lambda i: (i, 0)),
        ],
        out_specs=(
            pl.BlockSpec((B_HEADS, 1, t, hd), lambda i: (0, 0, i, 0)),
            pl.BlockSpec((B_HEADS, 1, t, hd), lambda i: (0, i // r, i % r, 0)),
            pl.BlockSpec((B_HEADS, 1, B_V, t), lambda i: (0, i // r, 0, i % r)),
        ),
        scratch_shapes=[pltpu.VMEM((t, B_HEADS * hd), F32), pltpu.VMEM((t, B_HEADS * hd), F32)],
        compiler_params=_cparams(("parallel",), 48),
        name="mla_prep",
    )(z, z, z, cqn, ckvn, wq, wkv, gq, gk, cos_t, sin_a, sin_b)


def _gqa_prep_kernel(xq_ref, xk_ref, xv_ref, gq_ref, gk_ref, cos_ref, sin_ref,
                     q_out, k_out, vt_out, *, scale):
    cos_c = cos_ref[...]
    sin_c = sin_ref[...]
    hd = C_HEAD_DIM
    grp = C_HEADS // C_KV_HEADS

    def norm_rope(x, g):
        xn = _rms_rows(x, g)
        return xn * cos_c + pltpu.roll(xn, hd // 2, 1) * sin_c

    gq = gq_ref[...]
    gk = gk_ref[...]
    for h in range(C_HEADS):
        qh = norm_rope(xq_ref[:, h * hd:(h + 1) * hd], gq) * scale
        q_out[h // grp, h % grp] = qh.astype(BF16)
    for h in range(C_KV_HEADS):
        k_out[h, 0] = norm_rope(xk_ref[:, h * hd:(h + 1) * hd], gk).astype(BF16)
        vt_out[h, 0] = xv_ref[:, h * hd:(h + 1) * hd].T.astype(BF16)


def _gqa_prep(z, gq, gk, cos_c, sin_c, *, scale, tk, q_col, t=256):
    s = z.shape[0]
    hd = C_HEAD_DIM
    grp = C_HEADS // C_KV_HEADS
    r = tk // t
    qw = C_HEADS * hd
    kw = C_KV_HEADS * hd
    const = lambda i: (0, 0)
    return pl.pallas_call(
        functools.partial(_gqa_prep_kernel, scale=scale),
        out_shape=(
            jax.ShapeDtypeStruct((C_KV_HEADS, grp, s, hd), BF16),
            jax.ShapeDtypeStruct((C_KV_HEADS, s // tk, tk, hd), BF16),
            jax.ShapeDtypeStruct((C_KV_HEADS, s // tk, hd, tk), BF16),
        ),
        grid=(s // t,),
        in_specs=[
            pl.BlockSpec((t, qw), lambda i: (i, q_col // qw)),
            pl.BlockSpec((t, kw), lambda i: (i, (q_col + qw) // kw)),
            pl.BlockSpec((t, kw), lambda i: (i, (q_col + qw) // kw + 1)),
            pl.BlockSpec((1, hd), const),
            pl.BlockSpec((1, hd), const),
            pl.BlockSpec((t, hd), lambda i: (i, 0)),
            pl.BlockSpec((t, hd), lambda i: (i, 0)),
        ],
        out_specs=(
            pl.BlockSpec((C_KV_HEADS, grp, t, hd), lambda i: (0, 0, i, 0)),
            pl.BlockSpec((C_KV_HEADS, 1, t, hd), lambda i: (0, i // r, i % r, 0)),
            pl.BlockSpec((C_KV_HEADS, 1, hd, t), lambda i: (0, i // r, 0, i % r)),
        ),
        compiler_params=_cparams(("parallel",), 32),
        name="gqa_prep",
    )(z, z, z, gq, gk, cos_c, sin_c)


def _attn_kernel(q_ref, k_ref, vt_ref, o_ref, m_ref, l_ref, acc_ref, *, grp, tq, dv, n_chunks):
    d = q_ref.shape[-1]
    q = q_ref[0].reshape(grp * tq, d)
    m_ref[...] = jnp.full(m_ref.shape, -jnp.inf, F32)
    l_ref[...] = jnp.zeros_like(l_ref)
    acc_ref[...] = jnp.zeros_like(acc_ref)

    def body(j, carry):
        k = k_ref[0, j]
        vt = vt_ref[0, j]
        st = lax.dot_general(k, q, (((1,), (1,)), ((), ())), preferred_element_type=F32)
        m_prev = m_ref[...]
        m_new = jnp.maximum(m_prev, jnp.max(st, axis=0, keepdims=True))
        alpha = jnp.exp(m_prev - m_new)
        p = jnp.exp(st - m_new)
        l_ref[...] = alpha * l_ref[...] + jnp.sum(p, axis=0, keepdims=True)
        acc_ref[...] = alpha * acc_ref[...] + jnp.dot(vt, p.astype(BF16), preferred_element_type=F32)
        m_ref[...] = m_new
        return carry

    lax.fori_loop(0, n_chunks, body, 0)
    o = (acc_ref[...] / l_ref[...]).T
    for g in range(grp):
        o_ref[:, g * dv:(g + 1) * dv] = o[g * tq:(g + 1) * tq].astype(o_ref.dtype)


def _attention(q, k, vt, *, tq):
    hk, grp, s, d = q.shape
    _, n_chunks, tk, _ = k.shape
    dv = vt.shape[2]
    gq = grp * tq
    return pl.pallas_call(
        functools.partial(_attn_kernel, grp=grp, tq=tq, dv=dv, n_chunks=n_chunks),
        out_shape=jax.ShapeDtypeStruct((s, hk * grp * dv), BF16),
        grid=(hk, s // tq),
        in_specs=[
            pl.BlockSpec((1, grp, tq, d), lambda h, i: (h, 0, i, 0)),
            pl.BlockSpec((1, n_chunks, tk, d), lambda h, i: (h, 0, 0, 0)),
            pl.BlockSpec((1, n_chunks, dv, tk), lambda h, i: (h, 0, 0, 0)),
        ],
        out_specs=pl.BlockSpec((tq, grp * dv), lambda h, i: (i, h)),
        scratch_shapes=[pltpu.VMEM((1, gq), F32), pltpu.VMEM((1, gq), F32), pltpu.VMEM((dv, gq), F32)],
        compiler_params=_cparams(("parallel", "arbitrary"), 48),
        name="attention",
    )(q, k, vt)


def _sgu_kernel(du_ref, dv_ref, g_ref, ws_ref, bias_ref, o_ref, vn_scr, *, t):
    vn_scr[...] = _rms_rows(_gelu_tanh(dv_ref[...]), g_ref[...]).astype(BF16)
    for c in range(t // D_CHUNK):
        rs = slice(c * D_CHUNK, (c + 1) * D_CHUNK)
        for g in range(D_GROUPS):
            cs = slice(g * V7X_LANES, (g + 1) * V7X_LANES)
            mixed = jnp.dot(ws_ref[g], vn_scr[rs, cs], preferred_element_type=F32)
            o_ref[rs, cs] = (_gelu_tanh(du_ref[rs, cs]) * (mixed + bias_ref[:, cs])).astype(o_ref.dtype)


def _sgu(z, g, ws, bias, *, col_blk, t=256):
    s = z.shape[0]
    w = g.shape[1]
    return pl.pallas_call(
        functools.partial(_sgu_kernel, t=t),
        out_shape=jax.ShapeDtypeStruct((s, w), BF16),
        grid=(s // t,),
        in_specs=[
            pl.BlockSpec((t, w), lambda i: (i, col_blk)),
            pl.BlockSpec((t, w), lambda i: (i, col_blk + 1)),
            pl.BlockSpec((1, w), lambda i: (0, 0)),
            pl.BlockSpec((D_GROUPS, D_CHUNK, D_CHUNK), lambda i: (0, 0, 0)),
            pl.BlockSpec((D_CHUNK, w), lambda i: (0, 0)),
        ],
        out_specs=pl.BlockSpec((t, w), lambda i: (i, 0)),
        scratch_shapes=[pltpu.VMEM((t, w), BF16)],
        compiler_params=_cparams(("parallel",), 32),
        name="sgu",
    )(z, z, g, ws, bias)


def _router_kernel(x_ref, g_ref, wr_ref, aff_ref):
    hn = _rms_rows(x_ref[...], g_ref[...])
    logits = lax.dot_general(wr_ref[...], hn, (((1,), (1,)), ((), ())),
                             precision=lax.Precision.HIGHEST, preferred_element_type=F32)
    e = jnp.exp(logits - jnp.max(logits, axis=0, keepdims=True))
    aff_ref[...] = e / jnp.sum(e, axis=0, keepdims=True)


def _router(x, g, wr_t, *, t=256):
    s, d = x.shape
    e = wr_t.shape[0]
    return pl.pallas_call(
        _router_kernel,
        out_shape=jax.ShapeDtypeStruct((e, s), F32),
        grid=(s // t,),
        in_specs=[
            pl.BlockSpec((t, d), lambda i: (i, 0)),
            pl.BlockSpec((1, d), lambda i: (0, 0)),
            pl.BlockSpec((e, d), lambda i: (0, 0)),
        ],
        out_specs=pl.BlockSpec((e, t), lambda i: (0, i)),
        compiler_params=_cparams(("parallel",), 32),
        name="router",
    )(x, g.reshape(1, d), wr_t)


def _cumsum_lanes(x, lane, s):
    k = 1
    while k < s:
        x = x + jnp.where(lane >= k, pltpu.roll(x, k, 1), 0)
        k *= 2
    return x


def _select_kernel(aff_ref, idx_ref, gate_ref, *, s, cap, nbits):
    aff = aff_ref[...]
    e = aff.shape[0]
    bits = pltpu.bitcast(aff, jnp.int32)
    capf = jnp.float32(cap)

    def count(mask):
        return jnp.sum(jnp.where(mask, 1.0, 0.0), axis=1, keepdims=True)

    def bs_body(i, thr):
        cand = thr | jnp.left_shift(jnp.int32(1), 30 - i)
        return jnp.where(count(bits >= cand) >= capf, cand, thr)

    thr = lax.fori_loop(0, 31, bs_body, jnp.zeros((e, 1), jnp.int32))
    lane = lax.broadcasted_iota(jnp.int32, (e, s), 1)
    above = bits > thr
    need = capf - count(above)
    tie = bits == thr
    tie_i = jnp.where(tie, 1, 0)
    tie_rank = _cumsum_lanes(tie_i, lane, s) - tie_i
    sel = above | (tie & (tie_rank.astype(F32) < need))
    sel_i = jnp.where(sel, 1, 0)
    pos = _cumsum_lanes(sel_i, lane, s) - sel_i
    disp = lane - pos
    valid_bit = 2 * nbits
    packed = jnp.where(sel, (1 << valid_bit) | (disp << nbits) | lane, 0)
    val = jnp.where(sel, bits, 0)

    def moving(pk, k):
        return ((pk >> (nbits + k)) & (pk >> valid_bit) & 1) == 1

    for k in range(nbits):
        sh = s - (1 << k)
        inc_p = pltpu.roll(packed, sh, 1)
        inc_v = pltpu.roll(val, sh, 1)
        inc = moving(inc_p, k)
        out = moving(packed, k)
        packed = jnp.where(inc, inc_p, jnp.where(out, 0, packed))
        val = jnp.where(inc, inc_v, jnp.where(out, 0, val))
    idx_ref[...] = packed[:, :cap] & (s - 1)
    gate_ref[...] = pltpu.bitcast(val[:, :cap], F32)


def _select(aff_t, cap):
    e, s = aff_t.shape
    nbits = s.bit_length() - 1
    assert 1 << nbits == s and 2 * nbits + 1 < 31
    return pl.pallas_call(
        functools.partial(_select_kernel, s=s, cap=cap, nbits=nbits),
        out_shape=(jax.ShapeDtypeStruct((e, cap), jnp.int32), jax.ShapeDtypeStruct((e, cap), F32)),
        in_specs=[pl.BlockSpec((e, s), lambda: (0, 0))],
        out_specs=(pl.BlockSpec((e, cap), lambda: (0, 0)), pl.BlockSpec((e, cap), lambda: (0, 0))),
        compiler_params=pltpu.CompilerParams(vmem_limit_bytes=32 * MIB),
        name="moe_select",
    )(aff_t)


def _row_copy(src, dst, sem, si, di):
    return pltpu.make_async_copy(src.at[pl.ds(si, 1)], dst.at[pl.ds(di, 1)], sem)


def _gather_norm_kernel(idx_ref, x_hbm, g_ref, o_ref, buf, sem, *, tm, rows):
    def issue(r, carry):
        _row_copy(x_hbm, buf, sem, idx_ref[0, 0, r], r).start()
        return carry

    lax.fori_loop(0, tm, issue, 0)

    def drain(r, carry):
        _row_copy(x_hbm, buf, sem, 0, r).wait()
        return carry

    lax.fori_loop(0, tm, drain, 0)
    g = g_ref[...]

    def body(c, carry):
        r0 = pl.multiple_of(c * rows, rows)
        o_ref[pl.ds(r0, rows), :] = _rms_rows(buf[pl.ds(r0, rows), :], g).astype(BF16)
        return carry

    lax.fori_loop(0, tm // rows, body, 0)


def _gather_norm(idx_tiles, x, g, *, tm):
    n_tiles = idx_tiles.shape[0]
    d = x.shape[1]
    return pl.pallas_call(
        functools.partial(_gather_norm_kernel, tm=tm, rows=min(64, tm)),
        out_shape=jax.ShapeDtypeStruct((n_tiles * tm, d), BF16),
        grid=(n_tiles,),
        in_specs=[
            pl.BlockSpec((1, 1, tm), lambda i: (i, 0, 0), memory_space=pltpu.SMEM),
            pl.BlockSpec(memory_space=pl.ANY),
            pl.BlockSpec((1, d), lambda i: (0, 0)),
        ],
        out_specs=pl.BlockSpec((tm, d), lambda i: (i, 0)),
        scratch_shapes=[pltpu.VMEM((tm, d), F32), pltpu.SemaphoreType.DMA],
        compiler_params=_cparams(("arbitrary",), 32),
        name="moe_gather_norm",
    )(idx_tiles, x, g.reshape(1, d))


def _ffn_a_kernel(x_ref, wg_ref, wu_ref, o_ref):
    x = x_ref[...]
    hg = jnp.dot(x, wg_ref[0].astype(BF16), preferred_element_type=F32)
    hu = jnp.dot(x, wu_ref[0].astype(BF16), preferred_element_type=F32)
    o_ref[...] = (hg * jax.nn.sigmoid(hg) * hu).astype(o_ref.dtype)


def _ffn_a(xg, w_gate, w_up, *, cap, tf=256):
    e, d, f = w_gate.shape
    return pl.pallas_call(
        _ffn_a_kernel,
        out_shape=jax.ShapeDtypeStruct((e * cap, f), BF16),
        grid=(e, f // tf),
        in_specs=[
            pl.BlockSpec((cap, d), lambda i, j: (i, 0), pipeline_mode=pl.Buffered(1)),
            pl.BlockSpec((1, d, tf), lambda i, j: (i, 0, j)),
            pl.BlockSpec((1, d, tf), lambda i, j: (i, 0, j)),
        ],
        out_specs=pl.BlockSpec((cap, tf), lambda i, j: (i, j)),
        compiler_params=_cparams(("parallel", "arbitrary"), 52),
        name="moe_ffn_a",
    )(xg, w_gate, w_up)


def _ffn_b_kernel(h_ref, wd_ref, gate_ref, o_ref):
    y = jnp.dot(h_ref[...], wd_ref[0].astype(BF16), preferred_element_type=F32)
    o_ref[...] = y * gate_ref[...]


def _ffn_b(h1, w_down, gate_col, *, cap, tn=512):
    e, f, d = w_down.shape
    return pl.pallas_call(
        _ffn_b_kernel,
        out_shape=jax.ShapeDtypeStruct((e * cap, d), F32),
        grid=(e, d // tn),
        in_specs=[
            pl.BlockSpec((cap, f), lambda i, j: (i, 0)),
            pl.BlockSpec((1, f, tn), lambda i, j: (i, 0, j)),
            pl.BlockSpec((cap, 1), lambda i, j: (i, 0)),
        ],
        out_specs=pl.BlockSpec((cap, tn), lambda i, j: (i, j)),
        compiler_params=_cparams(("parallel", "arbitrary"), 48),
        name="moe_ffn_b",
    )(h1, w_down, gate_col)


def _scatter_add_kernel(idx_ref, y_ref, x_hbm, o_hbm, buf, sem_in, sem_out, *, tm):
    del x_hbm

    def fetch(r, carry):
        _row_copy(o_hbm, buf, sem_in, idx_ref[0, 0, r], r).start()
        return carry

    lax.fori_loop(0, tm, fetch, 0)

    def fetch_wait(r, carry):
        _row_copy(o_hbm, buf, sem_in, 0, r).wait()
        return carry

    lax.fori_loop(0, tm, fetch_wait, 0)
    buf[...] = buf[...] + y_ref[...]

    def put(r, carry):
        _row_copy(buf, o_hbm, sem_out, r, idx_ref[0, 0, r]).start()
        return carry

    lax.fori_loop(0, tm, put, 0)

    def put_wait(r, carry):
        _row_copy(buf, o_hbm, sem_out, r, 0).wait()
        return carry

    lax.fori_loop(0, tm, put_wait, 0)


def _scatter_add(idx_tiles, y, x, *, tm):
    n_tiles = idx_tiles.shape[0]
    s, d = x.shape
    return pl.pallas_call(
        functools.partial(_scatter_add_kernel, tm=tm),
        out_shape=jax.ShapeDtypeStruct((s, d), F32),
        grid=(n_tiles,),
        in_specs=[
            pl.BlockSpec((1, 1, tm), lambda i: (i, 0, 0), memory_space=pltpu.SMEM),
            pl.BlockSpec((tm, d), lambda i: (i, 0)),
            pl.BlockSpec(memory_space=pl.ANY),
        ],
        out_specs=pl.BlockSpec(memory_space=pl.ANY),
        scratch_shapes=[pltpu.VMEM((tm, d), F32), pltpu.SemaphoreType.DMA, pltpu.SemaphoreType.DMA],
        input_output_aliases={2: 0},
        compiler_params=_cparams(("arbitrary",), 32),
        name="moe_scatter_add",
    )(idx_tiles, y, x)


def _moe(x, g, w_router, w_gate, w_up, w_down):
    s, d = x.shape
    e = w_router.shape[1]
    cap = EC_CAPACITY * s // e
    tm = min(512, cap)
    aff_t = _router(x, g, w_router.T)
    idx, gate = _select(aff_t, cap)
    idx_tiles = idx.reshape(e * cap // tm, 1, tm)
    xg = _gather_norm(idx_tiles, x, g, tm=tm)
    h1 = _ffn_a(xg, w_gate, w_up, cap=cap)
    y = _ffn_b(h1, w_down, gate.reshape(e * cap, 1), cap=cap)
    return _scatter_add(idx_tiles, y, x, tm=tm)


def _axial_angles(seq, rot_dim):
    rows = seq // GRID_W
    row = jnp.repeat(jnp.arange(rows, dtype=F32), GRID_W)
    col = jnp.tile(jnp.arange(GRID_W, dtype=F32), rows)
    n = rot_dim // 4
    freqs = jnp.power(ROPE_THETA, -jnp.arange(n, dtype=F32) / n)
    ang = jnp.concatenate([row[:, None] * freqs, col[:, None] * freqs], axis=-1)
    return jnp.cos(ang), jnp.sin(ang)


def _pad_cols(w, n):
    return jnp.pad(w, ((0, 0), (0, n - w.shape[1])))


def kernel(x, e_norm, e_w_in, a_conv_w, a_conv_b, a_gate_w, a_gate_b, a_lambda, b_cq_norm, b_ckv_norm, b_w_q_up, b_w_kv_up, b_q_norm, b_k_norm, e_w_out, o_norm, o_w_in, c_q_norm, c_k_norm, d_v_norm, d_w_s, d_b_s, o_w_out, f_norm, w_router, w_gate, w_up, w_down):
    bsz, seq, d_model = x.shape
    assert bsz == 1
    xs = x.reshape(seq, d_model)
    tk = min(512, seq)
    a_width = A_HEADS * A_BLK
    hd_b = 2 * B_NOPE

    n_in0 = -(-e_w_in.shape[2] // 512) * 512
    w_in0 = _pad_cols(e_w_in[0], n_in0).astype(BF16)
    z = _norm_mm(xs, e_norm[0], w_in0)

    gw = a_gate_w[0].transpose(0, 2, 3, 1, 4).reshape(2, A_HEADS, A_BLK, 2 * A_BLK).astype(BF16)
    cb = a_conv_b[0].reshape(1, a_width)
    hf = _rglru_dir(z, a_conv_w[0], cb, gw, a_gate_b[0], a_lambda[0], reverse=False)
    ya = _rglru_dir(z, a_conv_w[0], cb, gw, a_gate_b[0], a_lambda[0], reverse=True, hf=hf)

    cos_b, sin_b = _axial_angles(seq, B_ROPE)
    zeros32 = jnp.zeros_like(cos_b)
    zeros64 = jnp.zeros((seq, B_ROPE), F32)
    cos_t = jnp.concatenate([cos_b, cos_b, zeros64], axis=-1)
    sin_a = jnp.concatenate([-sin_b, zeros32, zeros64], axis=-1)
    sin_bt = jnp.concatenate([zeros32, sin_b, zeros64], axis=-1)
    wq = jnp.pad(b_w_q_up[0].reshape(B_QLORA, B_HEADS, B_NOPE + B_ROPE),
                 ((0, 0), (0, 0), (0, hd_b - B_NOPE - B_ROPE))).reshape(B_QLORA, B_HEADS * hd_b).astype(BF16)
    wkv = b_w_kv_up[0].astype(BF16)
    gq = _pad_cols(b_q_norm[0].reshape(1, -1), hd_b)
    gk = _pad_cols(b_k_norm[0].reshape(1, -1), hd_b)
    q, k, vt = _mla_prep(z, b_cq_norm[0].reshape(1, -1), b_ckv_norm[0].reshape(1, -1), wq, wkv, gq, gk,
                         cos_t, sin_a, sin_bt, scale=float((B_NOPE + B_ROPE) ** -0.5), tk=tk)
    yb = _attention(q, k, vt, tq=min(512, seq))

    w_out0 = e_w_out[0].astype(BF16)
    xs = _mm2_res(ya, yb, w_out0[:a_width], w_out0[a_width:], xs)
    xs = _moe(xs, f_norm[0], w_router[0], w_gate[0], w_up[0], w_down[0])

    d_width = D_GROUPS * V7X_LANES
    attn_cols = (C_HEADS + 2 * C_KV_HEADS) * C_HEAD_DIM
    w_in1 = jnp.concatenate([o_w_in[0][:, attn_cols:], o_w_in[0][:, :attn_cols]], axis=1).astype(BF16)
    z = _norm_mm(xs, o_norm[0], w_in1)
    cos_c, sin_c = _axial_angles(seq, C_HEAD_DIM)
    cos_cc = jnp.concatenate([cos_c, cos_c], axis=-1)
    sin_cc = jnp.concatenate([-sin_c, sin_c], axis=-1)
    q, k, vt = _gqa_prep(z, c_q_norm[0].reshape(1, -1), c_k_norm[0].reshape(1, -1), cos_cc, sin_cc,
                         scale=float(C_HEAD_DIM ** -0.5), tk=tk, q_col=2 * d_width)
    yc = _attention(q, k, vt, tq=min(128, seq))

    bias = jnp.broadcast_to(d_b_s[0].T[:, :, None], (D_CHUNK, D_GROUPS, V7X_LANES)).reshape(D_CHUNK, d_width)
    yd = _sgu(z, d_v_norm[0].reshape(1, -1), d_w_s[0].astype(BF16), bias, col_blk=0)

    w_out1 = o_w_out[0].astype(BF16)
    qw = C_HEADS * C_HEAD_DIM
    xs = _mm2_res(yc, yd, w_out1[:qw], w_out1[qw:], xs)
    xs = _moe(xs, f_norm[1], w_router[1], w_gate[1], w_up[1], w_down[1])
    return xs.reshape(bsz, seq, d_model)
```

```python
import functools

import jax
import jax.numpy as jnp
from jax import lax
from jax.experimental import pallas as pl
from jax.experimental.pallas import tpu as pltpu

F32 = jnp.float32
BF16 = jnp.bfloat16

GRID_W = 64
ROPE_THETA = 10000.0
NORM_EPS = 1e-6
A_HEADS = 16
A_BLK = 128
CONV_W = 4
RG_C = 8.0
B_HEADS = 16
B_NOPE = 128
B_ROPE = 64
B_V = 128
B_QLORA = 1024
B_KVLORA = 512
C_HEADS = 16
C_KV_HEADS = 4
C_HEAD_DIM = 128
D_GROUPS = 16
D_CHUNK = 128
N_EXPERTS = 16
EC_CAPACITY = 2

V7X_LANES = 128
V7X_SUBLANES = 8
V7X_MXU_DIM = 256
V7X_VMEM_BYTES = 64 * 1024 * 1024
V7X_BF16_SUBLANES = 16
MIB = 1024 * 1024

V_PAD_ROWS = V7X_BF16_SUBLANES
LOG2E = 1.4426950408889634
ATTN_UNROLL = 4
DMA_ISSUE_UNROLL = 8


def _cparams(dims, vmem_mib, flags=None):
    assert vmem_mib * MIB < V7X_VMEM_BYTES
    return pltpu.CompilerParams(dimension_semantics=dims, vmem_limit_bytes=vmem_mib * MIB, flags=flags)


def _rms_rows(x, g):
    ms = jnp.mean(x * x, axis=-1, keepdims=True)
    return x * lax.rsqrt(ms + NORM_EPS) * g


def _gelu_tanh(x):
    return 0.5 * x * (1.0 + jnp.tanh(0.7978845608028654 * (x + 0.044715 * (x * x * x))))


def _norm_mm_kernel(x_ref, g_ref, w_ref, o_ref, xn_ref, *, tm, rows):
    @pl.when(pl.program_id(1) == 0)
    def _():
        g = g_ref[...]

        def body(c, carry):
            r0 = pl.multiple_of(c * rows, rows)
            xn_ref[pl.ds(r0, rows), :] = _rms_rows(x_ref[pl.ds(r0, rows), :], g).astype(BF16)
            return carry

        lax.fori_loop(0, tm // rows, body, 0)

    o_ref[...] = jnp.dot(xn_ref[...], w_ref[...], preferred_element_type=F32)


def _norm_mm(x, g, w, *, tm=512, tn=512):
    s, d = x.shape
    n = w.shape[1]
    return pl.pallas_call(
        functools.partial(_norm_mm_kernel, tm=tm, rows=64),
        out_shape=jax.ShapeDtypeStruct((s, n), F32),
        grid=(s // tm, n // tn),
        in_specs=[
            pl.BlockSpec((tm, d), lambda i, j: (i, 0)),
            pl.BlockSpec((1, d), lambda i, j: (0, 0)),
            pl.BlockSpec((d, tn), lambda i, j: (0, j)),
        ],
        out_specs=pl.BlockSpec((tm, tn), lambda i, j: (i, j)),
        scratch_shapes=[pltpu.VMEM((tm, d), BF16)],
        compiler_params=_cparams(("parallel", "arbitrary"), 40),
        name="norm_mm",
    )(x, g.reshape(1, d), w)


def _mm2_res_kernel(a1_ref, a2_ref, w1_ref, w2_ref, r_ref, o_ref):
    acc = jnp.dot(a1_ref[...], w1_ref[...], preferred_element_type=F32)
    acc = acc + jnp.dot(a2_ref[...], w2_ref[...], preferred_element_type=F32)
    o_ref[...] = r_ref[...] + acc


def _mm2_res(a1, a2, w1, w2, res, *, tm=512, tn=1024):
    s, k1 = a1.shape
    k2 = a2.shape[1]
    n = w1.shape[1]
    return pl.pallas_call(
        _mm2_res_kernel,
        out_shape=jax.ShapeDtypeStruct((s, n), F32),
        grid=(s // tm, n // tn),
        in_specs=[
            pl.BlockSpec((tm, k1), lambda i, j: (i, 0)),
            pl.BlockSpec((tm, k2), lambda i, j: (i, 0)),
            pl.BlockSpec((k1, tn), lambda i, j: (0, j)),
            pl.BlockSpec((k2, tn), lambda i, j: (0, j)),
            pl.BlockSpec((tm, tn), lambda i, j: (i, j)),
        ],
        out_specs=pl.BlockSpec((tm, tn), lambda i, j: (i, j)),
        compiler_params=_cparams(("parallel", "arbitrary"), 40),
        name="mm2_res",
    )(a1, a2, w1, w2, res)


def _rglru_kernel(*refs, reverse, final, t, c):
    if final:
        (xa_ref, xp_ref, xn_ref, cw_ref, cb_ref, gw_ref, gb_ref, lam_ref, ga_ref, hf_ref,
         o_ref, a_scr, b_scr, carry) = refs
    else:
        (xa_ref, xp_ref, xn_ref, cw_ref, cb_ref, gw_ref, gb_ref, lam_ref,
         o_ref, a_scr, b_scr, carry) = refs
    i = pl.program_id(0)
    n_t = pl.num_programs(0)
    ti = (n_t - 1 - i) if reverse else i

    @pl.when(i == 0)
    def _():
        carry[...] = jnp.zeros_like(carry)

    x = xa_ref[...]
    xprev = jnp.where(ti > 0, xp_ref[...], 0.0)
    xnext = jnp.where(ti < n_t - 1, xn_ref[...], 0.0)
    xe = jnp.concatenate([xprev, x, xnext], axis=0)
    n = t + 2 * V7X_SUBLANES
    cw = cw_ref[...]
    xc = cb_ref[...] + cw[2:3] * x
    for j, sh in ((0, 2), (1, 1), (3, n - 1)):
        xc = xc + cw[j:j + 1] * pltpu.roll(xe, sh, 0)[V7X_SUBLANES:V7X_SUBLANES + t]
    a_scr[...] = xc

    z = -lam_ref[0]
    sp = jnp.maximum(z, 0.0) + jnp.log1p(jnp.exp(-jnp.abs(z)))
    gb = gb_ref[0]
    row = lax.broadcasted_iota(jnp.int32, (t, A_BLK), 0) & (V7X_SUBLANES - 1)
    for h in range(c // A_BLK):
        sl = slice(h * A_BLK, (h + 1) * A_BLK)
        xch = a_scr[:, sl]
        g2 = jnp.dot(xch.astype(BF16), gw_ref[0, h], preferred_element_type=F32)
        r = jax.nn.sigmoid(g2[:, :A_BLK] + gb[0:1, sl])
        ig = jax.nn.sigmoid(g2[:, A_BLK:] + gb[1:2, sl])
        log_a = (-RG_C) * r * sp[:, sl]
        a = jnp.exp(log_a)
        th = jnp.tanh(log_a)
        b = jnp.sqrt(-2.0 * th / (1.0 - th)) * (ig * xch)
        for s in (1, 2, 4):
            if reverse:
                a_s = pltpu.roll(a, t - s, 0)
                b_s = pltpu.roll(b, t - s, 0)
                m = row < V7X_SUBLANES - s
            else:
                a_s = pltpu.roll(a, s, 0)
                b_s = pltpu.roll(b, s, 0)
                m = row >= s
            b = jnp.where(m, a * b_s + b, b)
            a = jnp.where(m, a * a_s, a)
        a_scr[:, sl] = a
        b_scr[:, sl] = b

    nb = t // V7X_SUBLANES

    def blk(k, h):
        kk = (nb - 1 - k) if reverse else k
        r0 = pl.multiple_of(kk * V7X_SUBLANES, V7X_SUBLANES)
        hh = a_scr[pl.ds(r0, V7X_SUBLANES), :] * h + b_scr[pl.ds(r0, V7X_SUBLANES), :]
        b_scr[pl.ds(r0, V7X_SUBLANES), :] = hh
        edge = hh[0:1] if reverse else hh[V7X_SUBLANES - 1:V7X_SUBLANES]
        return jnp.broadcast_to(edge, (V7X_SUBLANES, c))

    carry[...] = lax.fori_loop(0, nb, blk, carry[...])

    if final:
        o_ref[...] = (_gelu_tanh(ga_ref[...]) * (hf_ref[...] + b_scr[...])).astype(o_ref.dtype)
    else:
        o_ref[...] = b_scr[...]


def _rglru_dir(z, cw, cb, gw, gb, lam, *, reverse, hf=None, t=256):
    s = z.shape[0]
    c = cw.shape[1]
    n_t = s // t
    t8 = t // V7X_SUBLANES
    last8 = s // V7X_SUBLANES - 1
    d = 1 if reverse else 0
    final = hf is not None

    def tmap(i):
        return (n_t - 1 - i) if reverse else i

    in_specs = [
        pl.BlockSpec((t, c), lambda i: (tmap(i), 0)),
        pl.BlockSpec((V7X_SUBLANES, c), lambda i: (jnp.maximum(tmap(i) * t8 - 1, 0), 0)),
        pl.BlockSpec((V7X_SUBLANES, c), lambda i: (jnp.minimum((tmap(i) + 1) * t8, last8), 0)),
        pl.BlockSpec((CONV_W, c), lambda i: (0, 0)),
        pl.BlockSpec((1, c), lambda i: (0, 0)),
        pl.BlockSpec((1, c // A_BLK, A_BLK, 2 * A_BLK), lambda i: (d, 0, 0, 0)),
        pl.BlockSpec((1, 2, c), lambda i: (d, 0, 0)),
        pl.BlockSpec((1, 1, c), lambda i: (d, 0, 0)),
    ]
    args = [z, z, z, cw, cb, gw, gb, lam.reshape(2, 1, c)]
    if final:
        in_specs += [pl.BlockSpec((t, c), lambda i: (tmap(i), 1)),
                     pl.BlockSpec((t, c), lambda i: (tmap(i), 0))]
        args += [z, hf]
    return pl.pallas_call(
        functools.partial(_rglru_kernel, reverse=reverse, final=final, t=t, c=c),
        out_shape=jax.ShapeDtypeStruct((s, c), BF16 if final else F32),
        grid=(n_t,),
        in_specs=in_specs,
        out_specs=pl.BlockSpec((t, c), lambda i: (tmap(i), 0)),
        scratch_shapes=[pltpu.VMEM((t, c), F32), pltpu.VMEM((t, c), F32),
                        pltpu.VMEM((V7X_SUBLANES, c), F32)],
        compiler_params=_cparams(("arbitrary",), 40),
        name="rglru_bwd" if reverse else "rglru_fwd",
    )(*args)


def _ones_row_block(t):
    row = lax.broadcasted_iota(jnp.int32, (V_PAD_ROWS, t), 0)
    return jnp.where(row == 0, 1.0, 0.0).astype(BF16)


def _mla_prep_kernel(cq_ref, ckv_ref, kr_ref, cqn_ref, ckvn_ref, wq_ref, wkv_ref, gq_ref, gk_ref,
                     cos_ref, sa_ref, sb_ref, q_out, k_out, vt_out, q_scr, kv_scr, *, scale):
    hd = 2 * B_NOPE
    inv_dim = 1.0 / (B_NOPE + B_ROPE)
    cqn = _rms_rows(cq_ref[...], cqn_ref[...]).astype(BF16)
    q_scr[...] = jnp.dot(cqn, wq_ref[...], preferred_element_type=F32)
    ckvn = _rms_rows(ckv_ref[...], ckvn_ref[...]).astype(BF16)
    kv_scr[...] = jnp.dot(ckvn, wkv_ref[...], preferred_element_type=F32)

    cos_t = cos_ref[...]
    sin_a = sa_ref[...]
    sin_b = sb_ref[...]

    def rope(rp):
        return rp * cos_t + pltpu.roll(rp, 96, 1) * sin_a + pltpu.roll(rp, 32, 1) * sin_b

    gq = gq_ref[...]
    gk = gk_ref[...]
    krp = kr_ref[...]
    ss_kr = jnp.sum(krp * krp, axis=-1, keepdims=True)
    krr = rope(krp * gk[:, B_NOPE:])
    ones_rows = _ones_row_block(krp.shape[0])
    for h in range(B_HEADS):
        qh = q_scr[:, h * hd:(h + 1) * hd]
        fq = lax.rsqrt(jnp.sum(qh * qh, axis=-1, keepdims=True) * inv_dim + NORM_EPS) * scale
        qn = qh * fq * gq
        q_out[h, 0] = jnp.concatenate([qn[:, :B_NOPE], rope(qn[:, B_NOPE:])], axis=-1).T.astype(BF16)
        kn = kv_scr[:, h * hd:h * hd + B_NOPE]
        v = kv_scr[:, h * hd + B_NOPE:(h + 1) * hd]
        fk = lax.rsqrt((jnp.sum(kn * kn, axis=-1, keepdims=True) + ss_kr) * inv_dim + NORM_EPS)
        k_out[h, 0] = jnp.concatenate([kn * fk * gk[:, :B_NOPE], krr * fk], axis=-1).astype(BF16)
        vt_out[h, 0, :B_V] = v.T.astype(BF16)
        vt_out[h, 0, B_V:] = ones_rows


def _mla_prep(z, cqn, ckvn, wq, wkv, gq, gk, cos_t, sin_a, sin_b, *, scale, tk, t=256):
    s = z.shape[0]
    hd = 2 * B_NOPE
    r = tk // t
    c0 = 2 * (A_HEADS * A_BLK)
    cq_blk = c0 // B_QLORA
    ckv_blk = (c0 + B_QLORA) // B_KVLORA
    kr_blk = (c0 + B_QLORA + B_KVLORA) // V7X_LANES
    const = lambda i: (0, 0)
    return pl.pallas_call(
        functools.partial(_mla_prep_kernel, scale=scale),
        out_shape=(
            jax.ShapeDtypeStruct((B_HEADS, 1, hd, s), BF16),
            jax.ShapeDtypeStruct((B_HEADS, s // tk, tk, hd), BF16),
            jax.ShapeDtypeStruct((B_HEADS, s // tk, B_V + V_PAD_ROWS, tk), BF16),
        ),
        grid=(s // t,),
        in_specs=[
            pl.BlockSpec((t, B_QLORA), lambda i: (i, cq_blk)),
            pl.BlockSpec((t, B_KVLORA), lambda i: (i, ckv_blk)),
            pl.BlockSpec((t, V7X_LANES), lambda i: (i, kr_blk)),
            pl.BlockSpec((1, B_QLORA), const),
            pl.BlockSpec((1, B_KVLORA), const),
            pl.BlockSpec((B_QLORA, B_HEADS * hd), const),
            pl.BlockSpec((B_KVLORA, B_HEADS * hd), const),
            pl.BlockSpec((1, hd), const),
            pl.BlockSpec((1, hd), const),
            pl.BlockSpec((t, V7X_LANES), lambda i: (i, 0)),
            pl.BlockSpec((t, V7X_LANES), lambda i: (i, 0)),
            pl.BlockSpec((t, V7X_LANES), lambda i: (i, 0)),
        ],
        out_specs=(
            pl.BlockSpec((B_HEADS, 1, hd, t), lambda i: (0, 0, 0, i)),
            pl.BlockSpec((B_HEADS, 1, t, hd), lambda i: (0, i // r, i % r, 0)),
            pl.BlockSpec((B_HEADS, 1, B_V + V_PAD_ROWS, t), lambda i: (0, i // r, 0, i % r)),
        ),
        scratch_shapes=[pltpu.VMEM((t, B_HEADS * hd), F32), pltpu.VMEM((t, B_HEADS * hd), F32)],
        compiler_params=_cparams(("parallel",), 48),
        name="mla_prep",
    )(z, z, z, cqn, ckvn, wq, wkv, gq, gk, cos_t, sin_a, sin_b)


def _gqa_prep_kernel(xq_ref, xk_ref, xv_ref, gq_ref, gk_ref, cos_ref, sin_ref,
                     q_out, k_out, vt_out, *, scale):
    cos_c = cos_ref[...]
    sin_c = sin_ref[...]
    hd = C_HEAD_DIM
    grp = C_HEADS // C_KV_HEADS

    def norm_rope(x, g):
        xn = _rms_rows(x, g)
        return xn * cos_c + pltpu.roll(xn, hd // 2, 1) * sin_c

    gq = gq_ref[...]
    gk = gk_ref[...]
    ones_rows = _ones_row_block(xq_ref.shape[0])
    for h in range(C_HEADS):
        qh = norm_rope(xq_ref[:, h * hd:(h + 1) * hd], gq) * scale
        q_out[h // grp, h % grp] = qh.T.astype(BF16)
    for h in range(C_KV_HEADS):
        k_out[h, 0] = norm_rope(xk_ref[:, h * hd:(h + 1) * hd], gk).astype(BF16)
        vt_out[h, 0, :hd] = xv_ref[:, h * hd:(h + 1) * hd].T.astype(BF16)
        vt_out[h, 0, hd:] = ones_rows


def _gqa_prep(z, gq, gk, cos_c, sin_c, *, scale, tk, q_col, t=256):
    s = z.shape[0]
    hd = C_HEAD_DIM
    grp = C_HEADS // C_KV_HEADS
    r = tk // t
    qw = C_HEADS * hd
    kw = C_KV_HEADS * hd
    const = lambda i: (0, 0)
    return pl.pallas_call(
        functools.partial(_gqa_prep_kernel, scale=scale),
        out_shape=(
            jax.ShapeDtypeStruct((C_KV_HEADS, grp, hd, s), BF16),
            jax.ShapeDtypeStruct((C_KV_HEADS, s // tk, tk, hd), BF16),
            jax.ShapeDtypeStruct((C_KV_HEADS, s // tk, hd + V_PAD_ROWS, tk), BF16),
        ),
        grid=(s // t,),
        in_specs=[
            pl.BlockSpec((t, qw), lambda i: (i, q_col // qw)),
            pl.BlockSpec((t, kw), lambda i: (i, (q_col + qw) // kw)),
            pl.BlockSpec((t, kw), lambda i: (i, (q_col + qw) // kw + 1)),
            pl.BlockSpec((1, hd), const),
            pl.BlockSpec((1, hd), const),
            pl.BlockSpec((t, hd), lambda i: (i, 0)),
            pl.BlockSpec((t, hd), lambda i: (i, 0)),
        ],
        out_specs=(
            pl.BlockSpec((C_KV_HEADS, grp, hd, t), lambda i: (0, 0, 0, i)),
            pl.BlockSpec((C_KV_HEADS, 1, t, hd), lambda i: (0, i // r, i % r, 0)),
            pl.BlockSpec((C_KV_HEADS, 1, hd + V_PAD_ROWS, t), lambda i: (0, i // r, 0, i % r)),
        ),
        compiler_params=_cparams(("parallel",), 32),
        name="gqa_prep",
    )(z, z, z, gq, gk, cos_c, sin_c)


def _attn_kernel(qt_ref, k_ref, vt_ref, o_ref, qt_scr, st0, st1, cm0, cm1, p0, p1, al0, al1, m_scr, acc_scr,
                 *, pieces, dv, n_chunks):
    assert n_chunks % 2 == 0
    bufs = ((st0, cm0, p0, al0), (st1, cm1, p1, al1))

    def scores(c, b):
        st_ref, cm_ref, _, _ = bufs[b]
        st = jnp.dot(k_ref[0, c], qt_scr[...], preferred_element_type=F32)
        st_ref[...] = st
        cm_ref[...] = jnp.max(st, axis=0, keepdims=True)

    def probs(b):
        st_ref, cm_ref, p_ref, al_ref = bufs[b]
        m_prev = m_scr[...]
        m_new = jnp.maximum(m_prev, cm_ref[...])
        al_ref[...] = jnp.exp2(m_prev - m_new)
        p_ref[...] = jnp.exp2(st_ref[...] - m_new).astype(BF16)
        m_scr[...] = m_new

    def values(c, b):
        _, _, p_ref, al_ref = bufs[b]
        acc_scr[...] = al_ref[...] * acc_scr[...] + jnp.dot(vt_ref[0, c], p_ref[...],
                                                            preferred_element_type=F32)

    unroll = ATTN_UNROLL if n_chunks % ATTN_UNROLL == 0 else 2

    def body(i, carry):
        for c in range(unroll):
            ch = unroll * i + c
            scores(jnp.minimum(ch + 1, n_chunks - 1), (c + 1) % 2)
            probs(c % 2)
            values(jnp.maximum(ch - 1, 0), (c + 1) % 2)
        return carry

    for sub in pieces:
        off = 0
        for g, r0, n in sub:
            qt_scr[:, off:off + n] = qt_ref[0, g, :, r0:r0 + n]
            off += n
        m_scr[...] = jnp.full(m_scr.shape, -jnp.inf, F32)
        acc_scr[...] = jnp.zeros_like(acc_scr)
        p1[...] = jnp.zeros_like(p1)
        al1[...] = jnp.ones_like(al1)
        scores(0, 0)
        lax.fori_loop(0, n_chunks // unroll, body, 0)
        values(n_chunks - 1, 1)
        acc = acc_scr[...]
        o = (acc[:dv] / acc[dv:dv + 1]).T
        off = 0
        for g, r0, n in sub:
            o_ref[r0:r0 + n, g * dv:(g + 1) * dv] = o[off:off + n].astype(o_ref.dtype)
            off += n


def _attention(qt, k, vt, *, dv, tq, pieces):
    hk, grp, d, s = qt.shape
    _, n_chunks, tk, _ = k.shape
    dvp = vt.shape[2]
    w = sum(n for _, _, n in pieces[0])
    return pl.pallas_call(
        functools.partial(_attn_kernel, pieces=pieces, dv=dv, n_chunks=n_chunks),
        out_shape=jax.ShapeDtypeStruct((s, hk * grp * dv), BF16),
        grid=(hk, s // tq),
        in_specs=[
            pl.BlockSpec((1, grp, d, tq), lambda h, i: (h, 0, 0, i)),
            pl.BlockSpec((1, n_chunks, tk, d), lambda h, i: (h, 0, 0, 0)),
            pl.BlockSpec((1, n_chunks, dvp, tk), lambda h, i: (h, 0, 0, 0)),
        ],
        out_specs=pl.BlockSpec((tq, grp * dv), lambda h, i: (i, h)),
        scratch_shapes=[pltpu.VMEM((d, w), BF16),
                        pltpu.VMEM((tk, w), F32), pltpu.VMEM((tk, w), F32),
                        pltpu.VMEM((1, w), F32), pltpu.VMEM((1, w), F32),
                        pltpu.VMEM((tk, w), BF16), pltpu.VMEM((tk, w), BF16),
                        pltpu.VMEM((1, w), F32), pltpu.VMEM((1, w), F32),
                        pltpu.VMEM((1, w), F32),
                        pltpu.VMEM((dvp, w), F32)],
        compiler_params=_cparams(("parallel", "arbitrary"), 48),
        name="attention",
    )(qt, k, vt)


def _sgu_kernel(du_ref, dv_ref, g_ref, ws_ref, bias_ref, o_ref, vn_scr, *, t):
    vn_scr[...] = _rms_rows(_gelu_tanh(dv_ref[...]), g_ref[...]).astype(BF16)
    for c in range(t // D_CHUNK):
        rs = slice(c * D_CHUNK, (c + 1) * D_CHUNK)
        for g in range(D_GROUPS):
            cs = slice(g * V7X_LANES, (g + 1) * V7X_LANES)
            mixed = jnp.dot(ws_ref[g], vn_scr[rs, cs], preferred_element_type=F32)
            o_ref[rs, cs] = (_gelu_tanh(du_ref[rs, cs]) * (mixed + bias_ref[:, cs])).astype(o_ref.dtype)


def _sgu(z, g, ws, bias, *, col_blk, t=256):
    s = z.shape[0]
    w = g.shape[1]
    return pl.pallas_call(
        functools.partial(_sgu_kernel, t=t),
        out_shape=jax.ShapeDtypeStruct((s, w), BF16),
        grid=(s // t,),
        in_specs=[
            pl.BlockSpec((t, w), lambda i: (i, col_blk)),
            pl.BlockSpec((t, w), lambda i: (i, col_blk + 1)),
            pl.BlockSpec((1, w), lambda i: (0, 0)),
            pl.BlockSpec((D_GROUPS, D_CHUNK, D_CHUNK), lambda i: (0, 0, 0)),
            pl.BlockSpec((D_CHUNK, w), lambda i: (0, 0)),
        ],
        out_specs=pl.BlockSpec((t, w), lambda i: (i, 0)),
        scratch_shapes=[pltpu.VMEM((t, w), BF16)],
        compiler_params=_cparams(("parallel",), 32),
        name="sgu",
    )(z, z, g, ws, bias)


def _router_kernel(x_ref, g_ref, wr_ref, aff_ref):
    hn = _rms_rows(x_ref[...], g_ref[...])
    logits = lax.dot_general(wr_ref[...], hn, (((1,), (1,)), ((), ())),
                             precision=lax.Precision.HIGHEST, preferred_element_type=F32)
    e = jnp.exp(logits - jnp.max(logits, axis=0, keepdims=True))
    aff_ref[...] = e / jnp.sum(e, axis=0, keepdims=True)


def _router(x, g, wr_t, *, t=256):
    s, d = x.shape
    e = wr_t.shape[0]
    return pl.pallas_call(
        _router_kernel,
        out_shape=jax.ShapeDtypeStruct((e, s), F32),
        grid=(s // t,),
        in_specs=[
            pl.BlockSpec((t, d), lambda i: (i, 0)),
            pl.BlockSpec((1, d), lambda i: (0, 0)),
            pl.BlockSpec((e, d), lambda i: (0, 0)),
        ],
        out_specs=pl.BlockSpec((e, t), lambda i: (0, i)),
        compiler_params=_cparams(("parallel",), 32),
        name="router",
    )(x, g.reshape(1, d), wr_t)


def _cumsum_lanes(x, lane, s):
    k = 1
    while k < s:
        x = x + jnp.where(lane >= k, pltpu.roll(x, k, 1), 0)
        k *= 2
    return x


def _select_kernel(aff_ref, idx_ref, gate_ref, *, s, cap, nbits):
    aff = aff_ref[...]
    e = aff.shape[0]
    bits = pltpu.bitcast(aff, jnp.int32)
    capf = jnp.float32(cap)

    def count(mask):
        return jnp.sum(jnp.where(mask, 1.0, 0.0), axis=1, keepdims=True)

    def bs_body(i, thr):
        cand = thr | jnp.left_shift(jnp.int32(1), 30 - i)
        return jnp.where(count(bits >= cand) >= capf, cand, thr)

    thr = lax.fori_loop(0, 31, bs_body, jnp.zeros((e, 1), jnp.int32))
    lane = lax.broadcasted_iota(jnp.int32, (e, s), 1)
    above = bits > thr
    need = capf - count(above)
    tie = bits == thr
    tie_i = jnp.where(tie, 1, 0)
    tie_rank = _cumsum_lanes(tie_i, lane, s) - tie_i
    sel = above | (tie & (tie_rank.astype(F32) < need))
    sel_i = jnp.where(sel, 1, 0)
    pos = _cumsum_lanes(sel_i, lane, s) - sel_i
    disp = lane - pos
    valid_bit = 2 * nbits
    packed = jnp.where(sel, (1 << valid_bit) | (disp << nbits) | lane, 0)
    val = jnp.where(sel, bits, 0)

    def moving(pk, k):
        return ((pk >> (nbits + k)) & (pk >> valid_bit) & 1) == 1

    for k in range(nbits):
        sh = s - (1 << k)
        inc_p = pltpu.roll(packed, sh, 1)
        inc_v = pltpu.roll(val, sh, 1)
        inc = moving(inc_p, k)
        out = moving(packed, k)
        packed = jnp.where(inc, inc_p, jnp.where(out, 0, packed))
        val = jnp.where(inc, inc_v, jnp.where(out, 0, val))
    idx_ref[...] = packed[:, :cap] & (s - 1)
    gate_ref[...] = pltpu.bitcast(val[:, :cap], F32)


def _select(aff_t, cap):
    e, s = aff_t.shape
    nbits = s.bit_length() - 1
    assert 1 << nbits == s and 2 * nbits + 1 < 31
    return pl.pallas_call(
        functools.partial(_select_kernel, s=s, cap=cap, nbits=nbits),
        out_shape=(jax.ShapeDtypeStruct((e, cap), jnp.int32), jax.ShapeDtypeStruct((e, cap), F32)),
        in_specs=[pl.BlockSpec((e, s), lambda: (0, 0))],
        out_specs=(pl.BlockSpec((e, cap), lambda: (0, 0)), pl.BlockSpec((e, cap), lambda: (0, 0))),
        compiler_params=pltpu.CompilerParams(vmem_limit_bytes=32 * MIB),
        name="moe_select",
    )(aff_t)


def _row_copy(src, dst, sem, si, di):
    return pltpu.make_async_copy(src.at[pl.ds(si, 1)], dst.at[pl.ds(di, 1)], sem)


def _gather_norm_kernel(idx_ref, idx_next_ref, x_hbm, g_ref, o_ref, buf, sem, *, tm, rows):
    i = pl.program_id(0)
    slot = i % 2

    def fetch(iref, s):
        def issue(r, carry):
            _row_copy(x_hbm, buf.at[s], sem.at[s], iref[0, 0, r], r).start()
            return carry

        lax.fori_loop(0, tm, issue, 0, unroll=DMA_ISSUE_UNROLL)

    @pl.when(i == 0)
    def _():
        fetch(idx_ref, 0)

    @pl.when(i + 1 < pl.num_programs(0))
    def _():
        fetch(idx_next_ref, 1 - slot)

    def drain(r, carry):
        _row_copy(x_hbm, buf.at[slot], sem.at[slot], 0, r).wait()
        return carry

    lax.fori_loop(0, tm, drain, 0, unroll=DMA_ISSUE_UNROLL)
    g = g_ref[...]

    def body(c, carry):
        r0 = pl.multiple_of(c * rows, rows)
        o_ref[pl.ds(r0, rows), :] = _rms_rows(buf[slot, pl.ds(r0, rows), :], g).astype(BF16)
        return carry

    lax.fori_loop(0, tm // rows, body, 0)


def _gather_norm(idx_tiles, x, g, *, tm):
    n_tiles = idx_tiles.shape[0]
    d = x.shape[1]
    return pl.pallas_call(
        functools.partial(_gather_norm_kernel, tm=tm, rows=min(64, tm)),
        out_shape=jax.ShapeDtypeStruct((n_tiles * tm, d), BF16),
        grid=(n_tiles,),
        in_specs=[
            pl.BlockSpec((1, 1, tm), lambda i: (i, 0, 0), memory_space=pltpu.SMEM),
            pl.BlockSpec((1, 1, tm), lambda i: (jnp.minimum(i + 1, n_tiles - 1), 0, 0),
                         memory_space=pltpu.SMEM),
            pl.BlockSpec(memory_space=pl.ANY),
            pl.BlockSpec((1, d), lambda i: (0, 0)),
        ],
        out_specs=pl.BlockSpec((tm, d), lambda i: (i, 0)),
        scratch_shapes=[pltpu.VMEM((2, tm, d), F32), pltpu.SemaphoreType.DMA((2,))],
        compiler_params=_cparams(("arbitrary",), 40),
        name="moe_gather_norm",
    )(idx_tiles, idx_tiles, x, g.reshape(1, d))


def _ffn_a_kernel(x_ref, wg_ref, wu_ref, o_ref):
    x = x_ref[...]
    hg = jnp.dot(x, wg_ref[0, 0].astype(BF16), preferred_element_type=F32)
    hu = jnp.dot(x, wu_ref[0, 0].astype(BF16), preferred_element_type=F32)
    o_ref[...] = (hg * jax.nn.sigmoid(hg) * hu).astype(o_ref.dtype)


def _ffn_a(xg, w_gate, w_up, *, layer, cap, tf=256):
    _, e, d, f = w_gate.shape
    return pl.pallas_call(
        _ffn_a_kernel,
        out_shape=jax.ShapeDtypeStruct((e * cap, f), BF16),
        grid=(e, f // tf),
        in_specs=[
            pl.BlockSpec((cap, d), lambda i, j: (i, 0), pipeline_mode=pl.Buffered(1)),
            pl.BlockSpec((1, 1, d, tf), lambda i, j: (layer, i, 0, j)),
            pl.BlockSpec((1, 1, d, tf), lambda i, j: (layer, i, 0, j)),
        ],
        out_specs=pl.BlockSpec((cap, tf), lambda i, j: (i, j)),
        compiler_params=_cparams(("parallel", "arbitrary"), 52),
        name="moe_ffn_a",
    )(xg, w_gate, w_up)


def _ffn_b_kernel(h_ref, wd_ref, gate_ref, o_ref):
    y = jnp.dot(h_ref[...], wd_ref[0, 0].astype(BF16), preferred_element_type=F32)
    o_ref[...] = y * gate_ref[...]


def _ffn_b(h1, w_down, gate_col, *, layer, cap, tn=512):
    _, e, f, d = w_down.shape
    return pl.pallas_call(
        _ffn_b_kernel,
        out_shape=jax.ShapeDtypeStruct((e * cap, d), F32),
        grid=(e, d // tn),
        in_specs=[
            pl.BlockSpec((cap, f), lambda i, j: (i, 0)),
            pl.BlockSpec((1, 1, f, tn), lambda i, j: (layer, i, 0, j)),
            pl.BlockSpec((cap, 1), lambda i, j: (i, 0)),
        ],
        out_specs=pl.BlockSpec((cap, tn), lambda i, j: (i, j)),
        compiler_params=_cparams(("parallel", "arbitrary"), 48),
        name="moe_ffn_b",
    )(h1, w_down, gate_col)


def _scatter_add_kernel(idx_ref, y_ref, x_hbm, o_hbm, buf, sem_in, sem_out, *, tm):
    del x_hbm
    half = tm // 2

    def rows_loop(fn, h):
        lax.fori_loop(h * half, (h + 1) * half, fn, 0, unroll=DMA_ISSUE_UNROLL)

    def fetch(h):
        def fn(r, carry):
            _row_copy(o_hbm, buf, sem_in.at[h], idx_ref[0, 0, r], r).start()
            return carry
        rows_loop(fn, h)

    def fetch_wait(h):
        def fn(r, carry):
            _row_copy(o_hbm, buf, sem_in.at[h], 0, r).wait()
            return carry
        rows_loop(fn, h)

    def put(h):
        def fn(r, carry):
            _row_copy(buf, o_hbm, sem_out.at[h], r, idx_ref[0, 0, r]).start()
            return carry
        rows_loop(fn, h)

    def put_wait(h):
        def fn(r, carry):
            _row_copy(buf, o_hbm, sem_out.at[h], r, 0).wait()
            return carry
        rows_loop(fn, h)

    fetch(0)
    fetch(1)
    for h in range(2):
        rs = pl.ds(h * half, half)
        fetch_wait(h)
        buf[rs, :] = buf[rs, :] + y_ref[rs, :]
        put(h)
    put_wait(0)
    put_wait(1)


def _scatter_add(idx_tiles, y, x, *, tm):
    n_tiles = idx_tiles.shape[0]
    s, d = x.shape
    return pl.pallas_call(
        functools.partial(_scatter_add_kernel, tm=tm),
        out_shape=jax.ShapeDtypeStruct((s, d), F32),
        grid=(n_tiles,),
        in_specs=[
            pl.BlockSpec((1, 1, tm), lambda i: (i, 0, 0), memory_space=pltpu.SMEM),
            pl.BlockSpec((tm, d), lambda i: (i, 0)),
            pl.BlockSpec(memory_space=pl.ANY),
        ],
        out_specs=pl.BlockSpec(memory_space=pl.ANY),
        scratch_shapes=[pltpu.VMEM((tm, d), F32), pltpu.SemaphoreType.DMA((2,)),
                        pltpu.SemaphoreType.DMA((2,))],
        input_output_aliases={2: 0},
        compiler_params=_cparams(("arbitrary",), 32),
        name="moe_scatter_add",
    )(idx_tiles, y, x)


def _moe(x, g, w_router, w_gate, w_up, w_down, *, layer):
    s, d = x.shape
    e = w_router.shape[1]
    cap = EC_CAPACITY * s // e
    tm = min(512, cap)
    aff_t = _router(x, g, w_router.T)
    idx, gate = _select(aff_t, cap)
    idx_tiles = idx.reshape(e * cap // tm, 1, tm)
    xg = _gather_norm(idx_tiles, x, g, tm=tm)
    h1 = _ffn_a(xg, w_gate, w_up, layer=layer, cap=cap)
    y = _ffn_b(h1, w_down, gate.reshape(e * cap, 1), layer=layer, cap=cap)
    return _scatter_add(idx_tiles, y, x, tm=tm)


def _axial_angles(seq, rot_dim):
    rows = seq // GRID_W
    row = jnp.repeat(jnp.arange(rows, dtype=F32), GRID_W)
    col = jnp.tile(jnp.arange(GRID_W, dtype=F32), rows)
    n = rot_dim // 4
    freqs = jnp.power(ROPE_THETA, -jnp.arange(n, dtype=F32) / n)
    ang = jnp.concatenate([row[:, None] * freqs, col[:, None] * freqs], axis=-1)
    return jnp.cos(ang), jnp.sin(ang)


def _pad_cols(w, n):
    return jnp.pad(w, ((0, 0), (0, n - w.shape[1])))


def kernel(x, e_norm, e_w_in, a_conv_w, a_conv_b, a_gate_w, a_gate_b, a_lambda, b_cq_norm, b_ckv_norm, b_w_q_up, b_w_kv_up, b_q_norm, b_k_norm, e_w_out, o_norm, o_w_in, c_q_norm, c_k_norm, d_v_norm, d_w_s, d_b_s, o_w_out, f_norm, w_router, w_gate, w_up, w_down):
    bsz, seq, d_model = x.shape
    assert bsz == 1
    xs = x.reshape(seq, d_model)
    tk = min(1024, seq // 2)
    a_width = A_HEADS * A_BLK
    hd_b = 2 * B_NOPE

    n_in0 = -(-e_w_in.shape[2] // 512) * 512
    w_in0 = _pad_cols(e_w_in[0], n_in0).astype(BF16)
    z = _norm_mm(xs, e_norm[0], w_in0)

    gw = a_gate_w[0].transpose(0, 2, 3, 1, 4).reshape(2, A_HEADS, A_BLK, 2 * A_BLK).astype(BF16)
    cb = a_conv_b[0].reshape(1, a_width)
    hf = _rglru_dir(z, a_conv_w[0], cb, gw, a_gate_b[0], a_lambda[0], reverse=False)
    ya = _rglru_dir(z, a_conv_w[0], cb, gw, a_gate_b[0], a_lambda[0], reverse=True, hf=hf)

    cos_b, sin_b = _axial_angles(seq, B_ROPE)
    zeros32 = jnp.zeros_like(cos_b)
    zeros64 = jnp.zeros((seq, B_ROPE), F32)
    cos_t = jnp.concatenate([cos_b, cos_b, zeros64], axis=-1)
    sin_a = jnp.concatenate([-sin_b, zeros32, zeros64], axis=-1)
    sin_bt = jnp.concatenate([zeros32, sin_b, zeros64], axis=-1)
    wq = jnp.pad(b_w_q_up[0].reshape(B_QLORA, B_HEADS, B_NOPE + B_ROPE),
                 ((0, 0), (0, 0), (0, hd_b - B_NOPE - B_ROPE))).reshape(B_QLORA, B_HEADS * hd_b).astype(BF16)
    wkv = b_w_kv_up[0].astype(BF16)
    gq = _pad_cols(b_q_norm[0].reshape(1, -1), hd_b)
    gk = _pad_cols(b_k_norm[0].reshape(1, -1), hd_b)
    q, k, vt = _mla_prep(z, b_cq_norm[0].reshape(1, -1), b_ckv_norm[0].reshape(1, -1), wq, wkv, gq, gk,
                         cos_t, sin_a, sin_bt, scale=float((B_NOPE + B_ROPE) ** -0.5) * LOG2E, tk=tk)
    sub = min(512, seq // 2)
    yb = _attention(q, k, vt, dv=B_V, tq=2 * sub, pieces=(((0, 0, sub),), ((0, sub, sub),)))

    w_out0 = e_w_out[0].astype(BF16)
    xs = _mm2_res(ya, yb, w_out0[:a_width], w_out0[a_width:], xs)
    xs = _moe(xs, f_norm[0], w_router[0], w_gate, w_up, w_down, layer=0)

    d_width = D_GROUPS * V7X_LANES
    attn_cols = (C_HEADS + 2 * C_KV_HEADS) * C_HEAD_DIM
    w_in1 = jnp.concatenate([o_w_in[0][:, attn_cols:], o_w_in[0][:, :attn_cols]], axis=1).astype(BF16)
    z = _norm_mm(xs, o_norm[0], w_in1)
    cos_c, sin_c = _axial_angles(seq, C_HEAD_DIM)
    cos_cc = jnp.concatenate([cos_c, cos_c], axis=-1)
    sin_cc = jnp.concatenate([-sin_c, sin_c], axis=-1)
    q, k, vt = _gqa_prep(z, c_q_norm[0].reshape(1, -1), c_k_norm[0].reshape(1, -1), cos_cc, sin_cc,
                         scale=float(C_HEAD_DIM ** -0.5) * LOG2E, tk=tk, q_col=2 * d_width)
    tq_c = min(256, seq)
    yc = _attention(q, k, vt, dv=C_HEAD_DIM, tq=tq_c,
                    pieces=(((0, 0, tq_c), (1, 0, tq_c)), ((2, 0, tq_c), (3, 0, tq_c))))

    bias = jnp.broadcast_to(d_b_s[0].T[:, :, None], (D_CHUNK, D_GROUPS, V7X_LANES)).reshape(D_CHUNK, d_width)
    yd = _sgu(z, d_v_norm[0].reshape(1, -1), d_w_s[0].astype(BF16), bias, col_blk=0)

    w_out1 = o_w_out[0].astype(BF16)
    qw = C_HEADS * C_HEAD_DIM
    xs = _mm2_res(yc, yd, w_out1[:qw], w_out1[qw:], xs)
    xs = _moe(xs, f_norm[1], w_router[1], w_gate, w_up, w_down, layer=1)
    return xs.reshape(bsz, seq, d_model)
```

```python
import functools

import jax
import jax.numpy as jnp
from jax import lax
from jax.experimental import pallas as pl
from jax.experimental.pallas import tpu as pltpu

F32 = jnp.float32
BF16 = jnp.bfloat16

GRID_W = 64
ROPE_THETA = 10000.0
NORM_EPS = 1e-6
A_HEADS = 16
A_BLK = 128
CONV_W = 4
RG_C = 8.0
B_HEADS = 16
B_NOPE = 128
B_ROPE = 64
B_V = 128
B_QLORA = 1024
B_KVLORA = 512
C_HEADS = 16
C_KV_HEADS = 4
C_HEAD_DIM = 128
D_GROUPS = 16
D_CHUNK = 128
N_EXPERTS = 16
EC_CAPACITY = 2

V7X_LANES = 128
V7X_SUBLANES = 8
V7X_MXU_DIM = 256
V7X_VMEM_BYTES = 64 * 1024 * 1024
V7X_BF16_SUBLANES = 16
MIB = 1024 * 1024

V_PAD_ROWS = V7X_BF16_SUBLANES
LOG2E = 1.4426950408889634
ATTN_UNROLL = 8
DMA_ISSUE_UNROLL = 8


def _cparams(dims, vmem_mib, flags=None):
    assert vmem_mib * MIB < V7X_VMEM_BYTES
    return pltpu.CompilerParams(dimension_semantics=dims, vmem_limit_bytes=vmem_mib * MIB, flags=flags)


def _rms_rows(x, g):
    ms = jnp.mean(x * x, axis=-1, keepdims=True)
    return x * lax.rsqrt(ms + NORM_EPS) * g


def _gelu_tanh(x):
    return 0.5 * x * (1.0 + jnp.tanh(0.7978845608028654 * (x + 0.044715 * (x * x * x))))


def _norm_mm_kernel(x_ref, g_ref, w_ref, o_ref, xn_ref, *, tm, rows):
    @pl.when(pl.program_id(1) == 0)
    def _():
        g = g_ref[...]

        def body(c, carry):
            r0 = pl.multiple_of(c * rows, rows)
            xn_ref[pl.ds(r0, rows), :] = _rms_rows(x_ref[pl.ds(r0, rows), :], g).astype(BF16)
            return carry

        lax.fori_loop(0, tm // rows, body, 0)

    o_ref[...] = jnp.dot(xn_ref[...], w_ref[...], preferred_element_type=F32)


def _norm_mm(x, g, w, *, tm=512, tn=512):
    s, d = x.shape
    n = w.shape[1]
    return pl.pallas_call(
        functools.partial(_norm_mm_kernel, tm=tm, rows=64),
        out_shape=jax.ShapeDtypeStruct((s, n), F32),
        grid=(s // tm, n // tn),
        in_specs=[
            pl.BlockSpec((tm, d), lambda i, j: (i, 0)),
            pl.BlockSpec((1, d), lambda i, j: (0, 0)),
            pl.BlockSpec((d, tn), lambda i, j: (0, j)),
        ],
        out_specs=pl.BlockSpec((tm, tn), lambda i, j: (i, j)),
        scratch_shapes=[pltpu.VMEM((tm, d), BF16)],
        compiler_params=_cparams(("parallel", "arbitrary"), 40),
        name="norm_mm",
    )(x, g.reshape(1, d), w)


def _mm2_res_kernel(a1_ref, a2_ref, w1_ref, w2_ref, r_ref, o_ref):
    acc = jnp.dot(a1_ref[...], w1_ref[...], preferred_element_type=F32)
    acc = acc + jnp.dot(a2_ref[...], w2_ref[...], preferred_element_type=F32)
    o_ref[...] = r_ref[...] + acc


def _mm2_res(a1, a2, w1, w2, res, *, tm=512, tn=1024):
    s, k1 = a1.shape
    k2 = a2.shape[1]
    n = w1.shape[1]
    return pl.pallas_call(
        _mm2_res_kernel,
        out_shape=jax.ShapeDtypeStruct((s, n), F32),
        grid=(s // tm, n // tn),
        in_specs=[
            pl.BlockSpec((tm, k1), lambda i, j: (i, 0)),
            pl.BlockSpec((tm, k2), lambda i, j: (i, 0)),
            pl.BlockSpec((k1, tn), lambda i, j: (0, j)),
            pl.BlockSpec((k2, tn), lambda i, j: (0, j)),
            pl.BlockSpec((tm, tn), lambda i, j: (i, j)),
        ],
        out_specs=pl.BlockSpec((tm, tn), lambda i, j: (i, j)),
        compiler_params=_cparams(("parallel", "arbitrary"), 40),
        name="mm2_res",
    )(a1, a2, w1, w2, res)


def _rglru_kernel(*refs, reverse, final, t, c):
    if final:
        (xa_ref, xp_ref, xn_ref, cw_ref, cb_ref, gw_ref, gb_ref, lam_ref, ga_ref, hf_ref,
         o_ref, a_scr, b_scr, carry) = refs
    else:
        (xa_ref, xp_ref, xn_ref, cw_ref, cb_ref, gw_ref, gb_ref, lam_ref,
         o_ref, a_scr, b_scr, carry) = refs
    i = pl.program_id(0)
    n_t = pl.num_programs(0)
    ti = (n_t - 1 - i) if reverse else i

    @pl.when(i == 0)
    def _():
        carry[...] = jnp.zeros_like(carry)

    x = xa_ref[...]
    xprev = jnp.where(ti > 0, xp_ref[...], 0.0)
    xnext = jnp.where(ti < n_t - 1, xn_ref[...], 0.0)
    xe = jnp.concatenate([xprev, x, xnext], axis=0)
    n = t + 2 * V7X_SUBLANES
    cw = cw_ref[...]
    xc = cb_ref[...] + cw[2:3] * x
    for j, sh in ((0, 2), (1, 1), (3, n - 1)):
        xc = xc + cw[j:j + 1] * pltpu.roll(xe, sh, 0)[V7X_SUBLANES:V7X_SUBLANES + t]
    a_scr[...] = xc

    z = -lam_ref[0]
    sp = jnp.maximum(z, 0.0) + jnp.log1p(jnp.exp(-jnp.abs(z)))
    gb = gb_ref[0]
    row = lax.broadcasted_iota(jnp.int32, (t, A_BLK), 0) & (V7X_SUBLANES - 1)
    for h in range(c // A_BLK):
        sl = slice(h * A_BLK, (h + 1) * A_BLK)
        xch = a_scr[:, sl]
        g2 = jnp.dot(xch.astype(BF16), gw_ref[0, h], preferred_element_type=F32)
        r = jax.nn.sigmoid(g2[:, :A_BLK] + gb[0:1, sl])
        ig = jax.nn.sigmoid(g2[:, A_BLK:] + gb[1:2, sl])
        log_a = (-RG_C) * r * sp[:, sl]
        a = jnp.exp(log_a)
        th = jnp.tanh(log_a)
        b = jnp.sqrt(-2.0 * th / (1.0 - th)) * (ig * xch)
        for s in (1, 2, 4):
            if reverse:
                a_s = pltpu.roll(a, t - s, 0)
                b_s = pltpu.roll(b, t - s, 0)
                m = row < V7X_SUBLANES - s
            else:
                a_s = pltpu.roll(a, s, 0)
                b_s = pltpu.roll(b, s, 0)
                m = row >= s
            b = jnp.where(m, a * b_s + b, b)
            a = jnp.where(m, a * a_s, a)
        a_scr[:, sl] = a
        b_scr[:, sl] = b

    nb = t // V7X_SUBLANES

    def blk(k, h):
        kk = (nb - 1 - k) if reverse else k
        r0 = pl.multiple_of(kk * V7X_SUBLANES, V7X_SUBLANES)
        hh = a_scr[pl.ds(r0, V7X_SUBLANES), :] * h + b_scr[pl.ds(r0, V7X_SUBLANES), :]
        b_scr[pl.ds(r0, V7X_SUBLANES), :] = hh
        edge = hh[0:1] if reverse else hh[V7X_SUBLANES - 1:V7X_SUBLANES]
        return jnp.broadcast_to(edge, (V7X_SUBLANES, c))

    carry[...] = lax.fori_loop(0, nb, blk, carry[...])

    if final:
        o_ref[...] = (_gelu_tanh(ga_ref[...]) * (hf_ref[...] + b_scr[...])).astype(o_ref.dtype)
    else:
        o_ref[...] = b_scr[...]


def _rglru_dir(z, cw, cb, gw, gb, lam, *, reverse, hf=None, t=256):
    s = z.shape[0]
    c = cw.shape[1]
    n_t = s // t
    t8 = t // V7X_SUBLANES
    last8 = s // V7X_SUBLANES - 1
    d = 1 if reverse else 0
    final = hf is not None

    def tmap(i):
        return (n_t - 1 - i) if reverse else i

    in_specs = [
        pl.BlockSpec((t, c), lambda i: (tmap(i), 0)),
        pl.BlockSpec((V7X_SUBLANES, c), lambda i: (jnp.maximum(tmap(i) * t8 - 1, 0), 0)),
        pl.BlockSpec((V7X_SUBLANES, c), lambda i: (jnp.minimum((tmap(i) + 1) * t8, last8), 0)),
        pl.BlockSpec((CONV_W, c), lambda i: (0, 0)),
        pl.BlockSpec((1, c), lambda i: (0, 0)),
        pl.BlockSpec((1, c // A_BLK, A_BLK, 2 * A_BLK), lambda i: (d, 0, 0, 0)),
        pl.BlockSpec((1, 2, c), lambda i: (d, 0, 0)),
        pl.BlockSpec((1, 1, c), lambda i: (d, 0, 0)),
    ]
    args = [z, z, z, cw, cb, gw, gb, lam.reshape(2, 1, c)]
    if final:
        in_specs += [pl.BlockSpec((t, c), lambda i: (tmap(i), 1)),
                     pl.BlockSpec((t, c), lambda i: (tmap(i), 0))]
        args += [z, hf]
    return pl.pallas_call(
        functools.partial(_rglru_kernel, reverse=reverse, final=final, t=t, c=c),
        out_shape=jax.ShapeDtypeStruct((s, c), BF16 if final else F32),
        grid=(n_t,),
        in_specs=in_specs,
        out_specs=pl.BlockSpec((t, c), lambda i: (tmap(i), 0)),
        scratch_shapes=[pltpu.VMEM((t, c), F32), pltpu.VMEM((t, c), F32),
                        pltpu.VMEM((V7X_SUBLANES, c), F32)],
        compiler_params=_cparams(("arbitrary",), 40),
        name="rglru_bwd" if reverse else "rglru_fwd",
    )(*args)


def _ones_row_block(t):
    row = lax.broadcasted_iota(jnp.int32, (V_PAD_ROWS, t), 0)
    return jnp.where(row == 0, 1.0, 0.0).astype(BF16)


def _mla_prep_kernel(cq_ref, ckv_ref, kr_ref, cqn_ref, ckvn_ref, wq_ref, wkv_ref, gq_ref, gk_ref,
                     cos_ref, sa_ref, sb_ref, q_out, k_out, vt_out, q_scr, kv_scr, *, scale):
    hd = 2 * B_NOPE
    inv_dim = 1.0 / (B_NOPE + B_ROPE)
    cqn = _rms_rows(cq_ref[...], cqn_ref[...]).astype(BF16)
    q_scr[...] = jnp.dot(cqn, wq_ref[...], preferred_element_type=F32)
    ckvn = _rms_rows(ckv_ref[...], ckvn_ref[...]).astype(BF16)
    kv_scr[...] = jnp.dot(ckvn, wkv_ref[...], preferred_element_type=F32)

    cos_t = cos_ref[...]
    sin_a = sa_ref[...]
    sin_b = sb_ref[...]

    def rope(rp):
        return rp * cos_t + pltpu.roll(rp, 96, 1) * sin_a + pltpu.roll(rp, 32, 1) * sin_b

    gq = gq_ref[...]
    gk = gk_ref[...]
    krp = kr_ref[...]
    ss_kr = jnp.sum(krp * krp, axis=-1, keepdims=True)
    krr = rope(krp * gk[:, B_NOPE:])
    ones_rows = _ones_row_block(krp.shape[0])
    for h in range(B_HEADS):
        qh = q_scr[:, h * hd:(h + 1) * hd]
        fq = lax.rsqrt(jnp.sum(qh * qh, axis=-1, keepdims=True) * inv_dim + NORM_EPS) * scale
        qn = qh * fq * gq
        q_out[h, 0] = jnp.concatenate([qn[:, :B_NOPE], rope(qn[:, B_NOPE:])], axis=-1).T.astype(BF16)
        kn = kv_scr[:, h * hd:h * hd + B_NOPE]
        v = kv_scr[:, h * hd + B_NOPE:(h + 1) * hd]
        fk = lax.rsqrt((jnp.sum(kn * kn, axis=-1, keepdims=True) + ss_kr) * inv_dim + NORM_EPS)
        k_out[h, 0] = jnp.concatenate([kn * fk * gk[:, :B_NOPE], krr * fk], axis=-1).astype(BF16)
        vt_out[h, 0, :B_V] = v.T.astype(BF16)
        vt_out[h, 0, B_V:] = ones_rows


def _mla_prep(z, cqn, ckvn, wq, wkv, gq, gk, cos_t, sin_a, sin_b, *, scale, tk, t=256):
    s = z.shape[0]
    hd = 2 * B_NOPE
    r = tk // t
    c0 = 2 * (A_HEADS * A_BLK)
    cq_blk = c0 // B_QLORA
    ckv_blk = (c0 + B_QLORA) // B_KVLORA
    kr_blk = (c0 + B_QLORA + B_KVLORA) // V7X_LANES
    const = lambda i: (0, 0)
    return pl.pallas_call(
        functools.partial(_mla_prep_kernel, scale=scale),
        out_shape=(
            jax.ShapeDtypeStruct((B_HEADS, 1, hd, s), BF16),
            jax.ShapeDtypeStruct((B_HEADS, s // tk, tk, hd), BF16),
            jax.ShapeDtypeStruct((B_HEADS, s // tk, B_V + V_PAD_ROWS, tk), BF16),
        ),
        grid=(s // t,),
        in_specs=[
            pl.BlockSpec((t, B_QLORA), lambda i: (i, cq_blk)),
            pl.BlockSpec((t, B_KVLORA), lambda i: (i, ckv_blk)),
            pl.BlockSpec((t, V7X_LANES), lambda i: (i, kr_blk)),
            pl.BlockSpec((1, B_QLORA), const),
            pl.BlockSpec((1, B_KVLORA), const),
            pl.BlockSpec((B_QLORA, B_HEADS * hd), const),
            pl.BlockSpec((B_KVLORA, B_HEADS * hd), const),
            pl.BlockSpec((1, hd), const),
            pl.BlockSpec((1, hd), const),
            pl.BlockSpec((t, V7X_LANES), lambda i: (i, 0)),
            pl.BlockSpec((t, V7X_LANES), lambda i: (i, 0)),
            pl.BlockSpec((t, V7X_LANES), lambda i: (i, 0)),
        ],
        out_specs=(
            pl.BlockSpec((B_HEADS, 1, hd, t), lambda i: (0, 0, 0, i)),
            pl.BlockSpec((B_HEADS, 1, t, hd), lambda i: (0, i // r, i % r, 0)),
            pl.BlockSpec((B_HEADS, 1, B_V + V_PAD_ROWS, t), lambda i: (0, i // r, 0, i % r)),
        ),
        scratch_shapes=[pltpu.VMEM((t, B_HEADS * hd), F32), pltpu.VMEM((t, B_HEADS * hd), F32)],
        compiler_params=_cparams(("parallel",), 48),
        name="mla_prep",
    )(z, z, z, cqn, ckvn, wq, wkv, gq, gk, cos_t, sin_a, sin_b)


def _gqa_prep_kernel(xq_ref, xk_ref, xv_ref, gq_ref, gk_ref, cos_ref, sin_ref,
                     q_out, k_out, vt_out, *, scale):
    cos_c = cos_ref[...]
    sin_c = sin_ref[...]
    hd = C_HEAD_DIM
    grp = C_HEADS // C_KV_HEADS

    def norm_rope(x, g):
        xn = _rms_rows(x, g)
        return xn * cos_c + pltpu.roll(xn, hd // 2, 1) * sin_c

    gq = gq_ref[...]
    gk = gk_ref[...]
    ones_rows = _ones_row_block(xq_ref.shape[0])
    for h in range(C_HEADS):
        qh = norm_rope(xq_ref[:, h * hd:(h + 1) * hd], gq) * scale
        q_out[h // grp, h % grp] = qh.T.astype(BF16)
    for h in range(C_KV_HEADS):
        k_out[h, 0] = norm_rope(xk_ref[:, h * hd:(h + 1) * hd], gk).astype(BF16)
        vt_out[h, 0, :hd] = xv_ref[:, h * hd:(h + 1) * hd].T.astype(BF16)
        vt_out[h, 0, hd:] = ones_rows


def _gqa_prep(z, gq, gk, cos_c, sin_c, *, scale, tk, q_col, t=256):
    s = z.shape[0]
    hd = C_HEAD_DIM
    grp = C_HEADS // C_KV_HEADS
    r = tk // t
    qw = C_HEADS * hd
    kw = C_KV_HEADS * hd
    const = lambda i: (0, 0)
    return pl.pallas_call(
        functools.partial(_gqa_prep_kernel, scale=scale),
        out_shape=(
            jax.ShapeDtypeStruct((C_KV_HEADS, grp, hd, s), BF16),
            jax.ShapeDtypeStruct((C_KV_HEADS, s // tk, tk, hd), BF16),
            jax.ShapeDtypeStruct((C_KV_HEADS, s // tk, hd + V_PAD_ROWS, tk), BF16),
        ),
        grid=(s // t,),
        in_specs=[
            pl.BlockSpec((t, qw), lambda i: (i, q_col // qw)),
            pl.BlockSpec((t, kw), lambda i: (i, (q_col + qw) // kw)),
            pl.BlockSpec((t, kw), lambda i: (i, (q_col + qw) // kw + 1)),
            pl.BlockSpec((1, hd), const),
            pl.BlockSpec((1, hd), const),
            pl.BlockSpec((t, hd), lambda i: (i, 0)),
            pl.BlockSpec((t, hd), lambda i: (i, 0)),
        ],
        out_specs=(
            pl.BlockSpec((C_KV_HEADS, grp, hd, t), lambda i: (0, 0, 0, i)),
            pl.BlockSpec((C_KV_HEADS, 1, t, hd), lambda i: (0, i // r, i % r, 0)),
            pl.BlockSpec((C_KV_HEADS, 1, hd + V_PAD_ROWS, t), lambda i: (0, i // r, 0, i % r)),
        ),
        compiler_params=_cparams(("parallel",), 32),
        name="gqa_prep",
    )(z, z, z, gq, gk, cos_c, sin_c)


def _attn_kernel(qt_ref, k_ref, vt_ref, o_ref, qt_scr, st0, st1, cm0, cm1, p0, p1, al0, al1, m_scr, acc_scr,
                 *, pieces, dv, n_chunks):
    assert n_chunks % 2 == 0
    bufs = ((st0, cm0, p0, al0), (st1, cm1, p1, al1))

    def scores(c, b):
        st_ref, cm_ref, _, _ = bufs[b]
        st = jnp.dot(k_ref[0, c], qt_scr[...], preferred_element_type=F32)
        st_ref[...] = st
        cm_ref[...] = jnp.max(st, axis=0, keepdims=True)

    def probs(b):
        st_ref, cm_ref, p_ref, al_ref = bufs[b]
        m_prev = m_scr[...]
        m_new = jnp.maximum(m_prev, cm_ref[...])
        al_ref[...] = jnp.exp2(m_prev - m_new)
        p_ref[...] = jnp.exp2(st_ref[...] - m_new).astype(BF16)
        m_scr[...] = m_new

    def values(c, b):
        _, _, p_ref, al_ref = bufs[b]
        acc_scr[...] = al_ref[...] * acc_scr[...] + jnp.dot(vt_ref[0, c], p_ref[...],
                                                            preferred_element_type=F32)

    unroll = ATTN_UNROLL if n_chunks % ATTN_UNROLL == 0 else 2

    def body(i, carry):
        for c in range(unroll):
            ch = unroll * i + c
            scores(jnp.minimum(ch + 1, n_chunks - 1), (c + 1) % 2)
            probs(c % 2)
            values(jnp.maximum(ch - 1, 0), (c + 1) % 2)
        return carry

    for sub in pieces:
        off = 0
        for g, r0, n in sub:
            qt_scr[:, off:off + n] = qt_ref[0, g, :, r0:r0 + n]
            off += n
        m_scr[...] = jnp.full(m_scr.shape, -jnp.inf, F32)
        acc_scr[...] = jnp.zeros_like(acc_scr)
        p1[...] = jnp.zeros_like(p1)
        al1[...] = jnp.ones_like(al1)
        scores(0, 0)
        lax.fori_loop(0, n_chunks // unroll, body, 0)
        values(n_chunks - 1, 1)
        acc = acc_scr[...]
        o = (acc[:dv] / acc[dv:dv + 1]).T
        off = 0
        for g, r0, n in sub:
            o_ref[r0:r0 + n, g * dv:(g + 1) * dv] = o[off:off + n].astype(o_ref.dtype)
            off += n


def _attention(qt, k, vt, *, dv, tq, pieces):
    hk, grp, d, s = qt.shape
    _, n_chunks, tk, _ = k.shape
    dvp = vt.shape[2]
    w = sum(n for _, _, n in pieces[0])
    return pl.pallas_call(
        functools.partial(_attn_kernel, pieces=pieces, dv=dv, n_chunks=n_chunks),
        out_shape=jax.ShapeDtypeStruct((s, hk * grp * dv), BF16),
        grid=(hk, s // tq),
        in_specs=[
            pl.BlockSpec((1, grp, d, tq), lambda h, i: (h, 0, 0, i)),
            pl.BlockSpec((1, n_chunks, tk, d), lambda h, i: (h, 0, 0, 0)),
            pl.BlockSpec((1, n_chunks, dvp, tk), lambda h, i: (h, 0, 0, 0)),
        ],
        out_specs=pl.BlockSpec((tq, grp * dv), lambda h, i: (i, h)),
        scratch_shapes=[pltpu.VMEM((d, w), BF16),
                        pltpu.VMEM((tk, w), F32), pltpu.VMEM((tk, w), F32),
                        pltpu.VMEM((1, w), F32), pltpu.VMEM((1, w), F32),
                        pltpu.VMEM((tk, w), BF16), pltpu.VMEM((tk, w), BF16),
                        pltpu.VMEM((1, w), F32), pltpu.VMEM((1, w), F32),
                        pltpu.VMEM((1, w), F32),
                        pltpu.VMEM((dvp, w), F32)],
        compiler_params=_cparams(("parallel", "arbitrary"), 48),
        name="attention",
    )(qt, k, vt)


def _sgu_kernel(du_ref, dv_ref, g_ref, ws_ref, bias_ref, o_ref, vn_scr, *, t):
    vn_scr[...] = _rms_rows(_gelu_tanh(dv_ref[...]), g_ref[...]).astype(BF16)
    for c in range(t // D_CHUNK):
        rs = slice(c * D_CHUNK, (c + 1) * D_CHUNK)
        for g in range(D_GROUPS):
            cs = slice(g * V7X_LANES, (g + 1) * V7X_LANES)
            mixed = jnp.dot(ws_ref[g], vn_scr[rs, cs], preferred_element_type=F32)
            o_ref[rs, cs] = (_gelu_tanh(du_ref[rs, cs]) * (mixed + bias_ref[:, cs])).astype(o_ref.dtype)


def _sgu(z, g, ws, bias, *, col_blk, t=256):
    s = z.shape[0]
    w = g.shape[1]
    return pl.pallas_call(
        functools.partial(_sgu_kernel, t=t),
        out_shape=jax.ShapeDtypeStruct((s, w), BF16),
        grid=(s // t,),
        in_specs=[
            pl.BlockSpec((t, w), lambda i: (i, col_blk)),
            pl.BlockSpec((t, w), lambda i: (i, col_blk + 1)),
            pl.BlockSpec((1, w), lambda i: (0, 0)),
            pl.BlockSpec((D_GROUPS, D_CHUNK, D_CHUNK), lambda i: (0, 0, 0)),
            pl.BlockSpec((D_CHUNK, w), lambda i: (0, 0)),
        ],
        out_specs=pl.BlockSpec((t, w), lambda i: (i, 0)),
        scratch_shapes=[pltpu.VMEM((t, w), BF16)],
        compiler_params=_cparams(("parallel",), 32),
        name="sgu",
    )(z, z, g, ws, bias)


def _router_kernel(x_ref, g_ref, wr_ref, aff_ref):
    hn = _rms_rows(x_ref[...], g_ref[...])
    logits = lax.dot_general(wr_ref[...], hn, (((1,), (1,)), ((), ())),
                             precision=lax.Precision.HIGHEST, preferred_element_type=F32)
    e = jnp.exp(logits - jnp.max(logits, axis=0, keepdims=True))
    aff_ref[...] = e / jnp.sum(e, axis=0, keepdims=True)


def _router(x, g, wr_t, *, t=256):
    s, d = x.shape
    e = wr_t.shape[0]
    return pl.pallas_call(
        _router_kernel,
        out_shape=jax.ShapeDtypeStruct((e, s), F32),
        grid=(s // t,),
        in_specs=[
            pl.BlockSpec((t, d), lambda i: (i, 0)),
            pl.BlockSpec((1, d), lambda i: (0, 0)),
            pl.BlockSpec((e, d), lambda i: (0, 0)),
        ],
        out_specs=pl.BlockSpec((e, t), lambda i: (0, i)),
        compiler_params=_cparams(("parallel",), 32),
        name="router",
    )(x, g.reshape(1, d), wr_t)


def _cumsum_lanes(x, lane, s):
    k = 1
    while k < s:
        x = x + jnp.where(lane >= k, pltpu.roll(x, k, 1), 0)
        k *= 2
    return x


def _select_kernel(aff_ref, idx_ref, gate_ref, *, s, cap, nbits):
    aff = aff_ref[...]
    e = aff.shape[0]
    bits = pltpu.bitcast(aff, jnp.int32)
    capf = jnp.float32(cap)

    def count(mask):
        return jnp.sum(jnp.where(mask, 1.0, 0.0), axis=1, keepdims=True)

    def bs_body(i, thr):
        cand = thr | jnp.left_shift(jnp.int32(1), 30 - i)
        return jnp.where(count(bits >= cand) >= capf, cand, thr)

    thr = lax.fori_loop(0, 31, bs_body, jnp.zeros((e, 1), jnp.int32))
    lane = lax.broadcasted_iota(jnp.int32, (e, s), 1)
    above = bits > thr
    need = capf - count(above)
    tie = bits == thr
    tie_i = jnp.where(tie, 1, 0)
    tie_rank = _cumsum_lanes(tie_i, lane, s) - tie_i
    sel = above | (tie & (tie_rank.astype(F32) < need))
    sel_i = jnp.where(sel, 1, 0)
    pos = _cumsum_lanes(sel_i, lane, s) - sel_i
    disp = lane - pos
    valid_bit = 2 * nbits
    packed = jnp.where(sel, (1 << valid_bit) | (disp << nbits) | lane, 0)
    val = jnp.where(sel, bits, 0)

    def moving(pk, k):
        return ((pk >> (nbits + k)) & (pk >> valid_bit) & 1) == 1

    for k in range(nbits):
        sh = s - (1 << k)
        inc_p = pltpu.roll(packed, sh, 1)
        inc_v = pltpu.roll(val, sh, 1)
        inc = moving(inc_p, k)
        out = moving(packed, k)
        packed = jnp.where(inc, inc_p, jnp.where(out, 0, packed))
        val = jnp.where(inc, inc_v, jnp.where(out, 0, val))
    idx_ref[...] = packed[:, :cap] & (s - 1)
    gate_ref[...] = pltpu.bitcast(val[:, :cap], F32)


def _select(aff_t, cap):
    e, s = aff_t.shape
    nbits = s.bit_length() - 1
    assert 1 << nbits == s and 2 * nbits + 1 < 31
    return pl.pallas_call(
        functools.partial(_select_kernel, s=s, cap=cap, nbits=nbits),
        out_shape=(jax.ShapeDtypeStruct((e, cap), jnp.int32), jax.ShapeDtypeStruct((e, cap), F32)),
        in_specs=[pl.BlockSpec((e, s), lambda: (0, 0))],
        out_specs=(pl.BlockSpec((e, cap), lambda: (0, 0)), pl.BlockSpec((e, cap), lambda: (0, 0))),
        compiler_params=pltpu.CompilerParams(vmem_limit_bytes=32 * MIB),
        name="moe_select",
    )(aff_t)


def _row_copy(src, dst, sem, si, di):
    return pltpu.make_async_copy(src.at[pl.ds(si, 1)], dst.at[pl.ds(di, 1)], sem)


def _gather_norm_kernel(idx_ref, idx_next_ref, x_hbm, g_ref, o_ref, buf, sem, *, tm, rows):
    i = pl.program_id(0)
    slot = i % 2

    def fetch(iref, s):
        def issue(r, carry):
            _row_copy(x_hbm, buf.at[s], sem.at[s], iref[0, 0, r], r).start()
            return carry

        lax.fori_loop(0, tm, issue, 0, unroll=DMA_ISSUE_UNROLL)

    @pl.when(i == 0)
    def _():
        fetch(idx_ref, 0)

    @pl.when(i + 1 < pl.num_programs(0))
    def _():
        fetch(idx_next_ref, 1 - slot)

    def drain(r, carry):
        _row_copy(x_hbm, buf.at[slot], sem.at[slot], 0, r).wait()
        return carry

    lax.fori_loop(0, tm, drain, 0, unroll=DMA_ISSUE_UNROLL)
    g = g_ref[...]

    def body(c, carry):
        r0 = pl.multiple_of(c * rows, rows)
        o_ref[pl.ds(r0, rows), :] = _rms_rows(buf[slot, pl.ds(r0, rows), :], g).astype(BF16)
        return carry

    lax.fori_loop(0, tm // rows, body, 0)


def _gather_norm(idx_tiles, x, g, *, tm):
    n_tiles = idx_tiles.shape[0]
    d = x.shape[1]
    return pl.pallas_call(
        functools.partial(_gather_norm_kernel, tm=tm, rows=min(64, tm)),
        out_shape=jax.ShapeDtypeStruct((n_tiles * tm, d), BF16),
        grid=(n_tiles,),
        in_specs=[
            pl.BlockSpec((1, 1, tm), lambda i: (i, 0, 0), memory_space=pltpu.SMEM),
            pl.BlockSpec((1, 1, tm), lambda i: (jnp.minimum(i + 1, n_tiles - 1), 0, 0),
                         memory_space=pltpu.SMEM),
            pl.BlockSpec(memory_space=pl.ANY),
            pl.BlockSpec((1, d), lambda i: (0, 0)),
        ],
        out_specs=pl.BlockSpec((tm, d), lambda i: (i, 0)),
        scratch_shapes=[pltpu.VMEM((2, tm, d), F32), pltpu.SemaphoreType.DMA((2,))],
        compiler_params=_cparams(("arbitrary",), 40),
        name="moe_gather_norm",
    )(idx_tiles, idx_tiles, x, g.reshape(1, d))


def _ffn_a_kernel(x_ref, wg_ref, wu_ref, o_ref):
    x = x_ref[...]
    hg = jnp.dot(x, wg_ref[0, 0].astype(BF16), preferred_element_type=F32)
    hu = jnp.dot(x, wu_ref[0, 0].astype(BF16), preferred_element_type=F32)
    o_ref[...] = (hg * jax.nn.sigmoid(hg) * hu).astype(o_ref.dtype)


def _ffn_a(xg, w_gate, w_up, *, layer, cap, tf=256):
    _, e, d, f = w_gate.shape
    return pl.pallas_call(
        _ffn_a_kernel,
        out_shape=jax.ShapeDtypeStruct((e * cap, f), BF16),
        grid=(e, f // tf),
        in_specs=[
            pl.BlockSpec((cap, d), lambda i, j: (i, 0), pipeline_mode=pl.Buffered(1)),
            pl.BlockSpec((1, 1, d, tf), lambda i, j: (layer, i, 0, j)),
            pl.BlockSpec((1, 1, d, tf), lambda i, j: (layer, i, 0, j)),
        ],
        out_specs=pl.BlockSpec((cap, tf), lambda i, j: (i, j)),
        compiler_params=_cparams(("parallel", "arbitrary"), 52),
        name="moe_ffn_a",
    )(xg, w_gate, w_up)


def _ffn_b_kernel(h_ref, wd_ref, gate_ref, o_ref):
    y = jnp.dot(h_ref[...], wd_ref[0, 0].astype(BF16), preferred_element_type=F32)
    o_ref[...] = y * gate_ref[...]


def _ffn_b(h1, w_down, gate_col, *, layer, cap, tn=512):
    _, e, f, d = w_down.shape
    return pl.pallas_call(
        _ffn_b_kernel,
        out_shape=jax.ShapeDtypeStruct((e * cap, d), F32),
        grid=(e, d // tn),
        in_specs=[
            pl.BlockSpec((cap, f), lambda i, j: (i, 0)),
            pl.BlockSpec((1, 1, f, tn), lambda i, j: (layer, i, 0, j)),
            pl.BlockSpec((cap, 1), lambda i, j: (i, 0)),
        ],
        out_specs=pl.BlockSpec((cap, tn), lambda i, j: (i, j)),
        compiler_params=_cparams(("parallel", "arbitrary"), 48),
        name="moe_ffn_b",
    )(h1, w_down, gate_col)


def _scatter_add_kernel(idx_ref, y_ref, x_hbm, o_hbm, buf, sem_in, sem_out, *, tm):
    del x_hbm
    half = tm // 2

    def rows_loop(fn, h):
        lax.fori_loop(h * half, (h + 1) * half, fn, 0, unroll=DMA_ISSUE_UNROLL)

    def fetch(h):
        def fn(r, carry):
            _row_copy(o_hbm, buf, sem_in.at[h], idx_ref[0, 0, r], r).start()
            return carry
        rows_loop(fn, h)

    def fetch_wait(h):
        def fn(r, carry):
            _row_copy(o_hbm, buf, sem_in.at[h], 0, r).wait()
            return carry
        rows_loop(fn, h)

    def put(h):
        def fn(r, carry):
            _row_copy(buf, o_hbm, sem_out.at[h], r, idx_ref[0, 0, r]).start()
            return carry
        rows_loop(fn, h)

    def put_wait(h):
        def fn(r, carry):
            _row_copy(buf, o_hbm, sem_out.at[h], r, 0).wait()
            return carry
        rows_loop(fn, h)

    fetch(0)
    fetch(1)
    for h in range(2):
        rs = pl.ds(h * half, half)
        fetch_wait(h)
        buf[rs, :] = buf[rs, :] + y_ref[rs, :]
        put(h)
    put_wait(0)
    put_wait(1)


def _scatter_add(idx_tiles, y, x, *, tm):
    n_tiles = idx_tiles.shape[0]
    s, d = x.shape
    return pl.pallas_call(
        functools.partial(_scatter_add_kernel, tm=tm),
        out_shape=jax.ShapeDtypeStruct((s, d), F32),
        grid=(n_tiles,),
        in_specs=[
            pl.BlockSpec((1, 1, tm), lambda i: (i, 0, 0), memory_space=pltpu.SMEM),
            pl.BlockSpec((tm, d), lambda i: (i, 0)),
            pl.BlockSpec(memory_space=pl.ANY),
        ],
        out_specs=pl.BlockSpec(memory_space=pl.ANY),
        scratch_shapes=[pltpu.VMEM((tm, d), F32), pltpu.SemaphoreType.DMA((2,)),
                        pltpu.SemaphoreType.DMA((2,))],
        input_output_aliases={2: 0},
        compiler_params=_cparams(("arbitrary",), 32),
        name="moe_scatter_add",
    )(idx_tiles, y, x)


def _moe(x, g, w_router, w_gate, w_up, w_down, *, layer):
    s, d = x.shape
    e = w_router.shape[1]
    cap = EC_CAPACITY * s // e
    tm = min(512, cap)
    aff_t = _router(x, g, w_router.T)
    idx, gate = _select(aff_t, cap)
    idx_tiles = idx.reshape(e * cap // tm, 1, tm)
    xg = _gather_norm(idx_tiles, x, g, tm=tm)
    h1 = _ffn_a(xg, w_gate, w_up, layer=layer, cap=cap)
    y = _ffn_b(h1, w_down, gate.reshape(e * cap, 1), layer=layer, cap=cap)
    return _scatter_add(idx_tiles, y, x, tm=tm)


def _axial_angles(seq, rot_dim):
    rows = seq // GRID_W
    row = jnp.repeat(jnp.arange(rows, dtype=F32), GRID_W)
    col = jnp.tile(jnp.arange(GRID_W, dtype=F32), rows)
    n = rot_dim // 4
    freqs = jnp.power(ROPE_THETA, -jnp.arange(n, dtype=F32) / n)
    ang = jnp.concatenate([row[:, None] * freqs, col[:, None] * freqs], axis=-1)
    return jnp.cos(ang), jnp.sin(ang)


def _pad_cols(w, n):
    return jnp.pad(w, ((0, 0), (0, n - w.shape[1])))


def kernel(x, e_norm, e_w_in, a_conv_w, a_conv_b, a_gate_w, a_gate_b, a_lambda, b_cq_norm, b_ckv_norm, b_w_q_up, b_w_kv_up, b_q_norm, b_k_norm, e_w_out, o_norm, o_w_in, c_q_norm, c_k_norm, d_v_norm, d_w_s, d_b_s, o_w_out, f_norm, w_router, w_gate, w_up, w_down):
    bsz, seq, d_model = x.shape
    assert bsz == 1
    xs = x.reshape(seq, d_model)
    tk = min(1024, seq // 2)
    a_width = A_HEADS * A_BLK
    hd_b = 2 * B_NOPE

    n_in0 = -(-e_w_in.shape[2] // 512) * 512
    w_in0 = _pad_cols(e_w_in[0], n_in0).astype(BF16)
    z = _norm_mm(xs, e_norm[0], w_in0)

    gw = a_gate_w[0].transpose(0, 2, 3, 1, 4).reshape(2, A_HEADS, A_BLK, 2 * A_BLK).astype(BF16)
    cb = a_conv_b[0].reshape(1, a_width)
    hf = _rglru_dir(z, a_conv_w[0], cb, gw, a_gate_b[0], a_lambda[0], reverse=False)
    ya = _rglru_dir(z, a_conv_w[0], cb, gw, a_gate_b[0], a_lambda[0], reverse=True, hf=hf)

    cos_b, sin_b = _axial_angles(seq, B_ROPE)
    zeros32 = jnp.zeros_like(cos_b)
    zeros64 = jnp.zeros((seq, B_ROPE), F32)
    cos_t = jnp.concatenate([cos_b, cos_b, zeros64], axis=-1)
    sin_a = jnp.concatenate([-sin_b, zeros32, zeros64], axis=-1)
    sin_bt = jnp.concatenate([zeros32, sin_b, zeros64], axis=-1)
    wq = jnp.pad(b_w_q_up[0].reshape(B_QLORA, B_HEADS, B_NOPE + B_ROPE),
                 ((0, 0), (0, 0), (0, hd_b - B_NOPE - B_ROPE))).reshape(B_QLORA, B_HEADS * hd_b).astype(BF16)
    wkv = b_w_kv_up[0].astype(BF16)
    gq = _pad_cols(b_q_norm[0].reshape(1, -1), hd_b)
    gk = _pad_cols(b_k_norm[0].reshape(1, -1), hd_b)
    q, k, vt = _mla_prep(z, b_cq_norm[0].reshape(1, -1), b_ckv_norm[0].reshape(1, -1), wq, wkv, gq, gk,
                         cos_t, sin_a, sin_bt, scale=float((B_NOPE + B_ROPE) ** -0.5) * LOG2E, tk=tk)
    sub = min(512, seq // 2)
    yb = _attention(q, k, vt, dv=B_V, tq=2 * sub, pieces=(((0, 0, sub),), ((0, sub, sub),)))

    w_out0 = e_w_out[0].astype(BF16)
    xs = _mm2_res(ya, yb, w_out0[:a_width], w_out0[a_width:], xs)
    xs = _moe(xs, f_norm[0], w_router[0], w_gate, w_up, w_down, layer=0)

    d_width = D_GROUPS * V7X_LANES
    attn_cols = (C_HEADS + 2 * C_KV_HEADS) * C_HEAD_DIM
    w_in1 = jnp.concatenate([o_w_in[0][:, attn_cols:], o_w_in[0][:, :attn_cols]], axis=1).astype(BF16)
    z = _norm_mm(xs, o_norm[0], w_in1)
    cos_c, sin_c = _axial_angles(seq, C_HEAD_DIM)
    cos_cc = jnp.concatenate([cos_c, cos_c], axis=-1)
    sin_cc = jnp.concatenate([-sin_c, sin_c], axis=-1)
    q, k, vt = _gqa_prep(z, c_q_norm[0].reshape(1, -1), c_k_norm[0].reshape(1, -1), cos_cc, sin_cc,
                         scale=float(C_HEAD_DIM ** -0.5) * LOG2E, tk=tk, q_col=2 * d_width)
    tq_c = min(256, seq)
    yc = _attention(q, k, vt, dv=C_HEAD_DIM, tq=tq_c,
                    pieces=(((0, 0, tq_c), (1, 0, tq_c)), ((2, 0, tq_c), (3, 0, tq_c))))

    bias = jnp.broadcast_to(d_b_s[0].T[:, :, None], (D_CHUNK, D_GROUPS, V7X_LANES)).reshape(D_CHUNK, d_width)
    yd = _sgu(z, d_v_norm[0].reshape(1, -1), d_w_s[0].astype(BF16), bias, col_blk=0)

    w_out1 = o_w_out[0].astype(BF16)
    qw = C_HEADS * C_HEAD_DIM
    xs = _mm2_res(yc, yd, w_out1[:qw], w_out1[qw:], xs)
    xs = _moe(xs, f_norm[1], w_router[1], w_gate, w_up, w_down, layer=1)
    return xs.reshape(bsz, seq, d_model)
```
